```python
import jax, jax.numpy as jnp
from jax import lax
import numpy as np


D_MODEL = 1024
BATCH = 2
SEQ = 8192
DEPTH = 2
DEC_BATCH = 32
DEC_SEQ = 8
PAST_LEN = 8192
PAGE_SIZE = 128

POOL_WIDTH = D_MODEL // 2
POOL_GROUPS = 4
POOL_GROUP_DIM = POOL_WIDTH // POOL_GROUPS
POOL_WINDOWS = (2, 4, 8, 16)
POOL_BUF = max(POOL_WINDOWS) - 1
ATTN_WIDTH = D_MODEL - POOL_WIDTH
HEAD_DIM = 64
N_HEADS = ATTN_WIDTH // HEAD_DIM
N_KV_HEADS = N_HEADS
ROT_DIM = HEAD_DIM // 4
IDX_HEADS = 4
IDX_DIM = 64
IDX_ROT_DIM = IDX_DIM // 4
TOPK_MAX = 256
ROPE_THETA = 500000.0
D_FF = 4 * D_MODEL
Q_BLOCK = 128
EPS = 1e-6
IN_SIZES = (POOL_WIDTH, N_HEADS * HEAD_DIM, N_KV_HEADS * HEAD_DIM, N_KV_HEADS * HEAD_DIM,
            IDX_HEADS * IDX_DIM, IDX_DIM, IDX_HEADS)
D_IN = sum(IN_SIZES)

kernel_name = 'hymba_pool_dsa_decoder_step'


def rmsnorm(x, g):
    xf = x.astype(jnp.float32)
    r = lax.rsqrt(jnp.mean(xf * xf, axis=-1, keepdims=True) + EPS)
    return (xf * r).astype(x.dtype) * g


def rope_partial(x, pos, rot_dim):
    half = rot_dim // 2
    inv = jnp.power(jnp.float32(ROPE_THETA), -jnp.arange(half, dtype=jnp.float32) * 2.0 / rot_dim)
    ang = pos.astype(jnp.float32)[:, None] * inv[None, :]
    cos = jnp.cos(ang)[None, :, None, :].astype(x.dtype)
    sin = jnp.sin(ang)[None, :, None, :].astype(x.dtype)
    x1 = x[..., :half]
    x2 = x[..., half:rot_dim]
    return jnp.concatenate([x1 * cos - x2 * sin, x2 * cos + x1 * sin, x[..., rot_dim:]], axis=-1)


def project(h, w_in, pos):
    B, T, _ = h.shape
    z = h @ w_in
    splits = np.cumsum(IN_SIZES)[:-1].tolist()
    u, q, k, v, qi, ki, wi = jnp.split(z, splits, axis=-1)
    q = rope_partial(q.reshape(B, T, N_HEADS, HEAD_DIM), pos, ROT_DIM)
    k = rope_partial(k.reshape(B, T, N_KV_HEADS, HEAD_DIM), pos, ROT_DIM)
    v = v.reshape(B, T, N_KV_HEADS, HEAD_DIM)
    qi = rope_partial(qi.reshape(B, T, IDX_HEADS, IDX_DIM), pos, IDX_ROT_DIM)
    ki = rope_partial(ki.reshape(B, T, 1, IDX_DIM), pos, IDX_ROT_DIM)[:, :, 0]
    return u, q, k, v, qi, ki, wi


def pool_mix(u_ext, n_prefix, pos0, w_grp, scale):
    B, n_ext, _ = u_ext.shape
    T = n_ext - n_prefix
    windows = jnp.array(POOL_WINDOWS, dtype=jnp.int32)
    uf = u_ext.astype(jnp.float32).reshape(B, n_ext, POOL_GROUPS, POOL_GROUP_DIM)
    cs = jnp.concatenate([jnp.zeros_like(uf[:, :1]), jnp.cumsum(uf, axis=1)], axis=1)
    e = n_prefix + jnp.arange(T)
    lo = jnp.maximum(e[:, None] - windows[None, :] + 1, 0)
    grp = jnp.arange(POOL_GROUPS)[None, :]
    win_sum = cs[:, e + 1] - cs[:, lo, grp]
    cnt = jnp.minimum(windows[None, :], pos0 + jnp.arange(T)[:, None] + 1).astype(jnp.float32)
    d = win_sum / cnt[None, :, :, None] - uf[:, n_prefix:]
    y = jnp.einsum('btgc,gcd->btgd', d.astype(u_ext.dtype), w_grp)
    y = y * scale.reshape(POOL_GROUPS, POOL_GROUP_DIM)
    return y.reshape(B, T, POOL_WIDTH)


def indexer_scores(qi, wi, ki_all):
    dots = jnp.einsum('bqhd,bld->bqhl', qi.astype(jnp.float32), ki_all.astype(jnp.float32))
    w = wi.astype(jnp.float32) * (IDX_HEADS ** -0.5 * IDX_DIM ** -0.5)
    return jnp.einsum('bqh,bqhl->bql', w, jax.nn.relu(dots))


def sparse_attend(q, k_sel, v_sel, valid):
    s = jnp.einsum('bqhd,bqkhd->bqhk', q, k_sel).astype(jnp.float32) * (HEAD_DIM ** -0.5)
    s = jnp.where(valid[:, :, None, :], s, -jnp.inf)
    p = jax.nn.softmax(s, axis=-1).astype(v_sel.dtype)
    return jnp.einsum('bqhk,bqkhd->bqhd', p, v_sel)


def prompt_sparse_attention(q, k, v, qi, ki, wi):
    B, S = q.shape[:2]
    topk = min(TOPK_MAX, S // 4)
    nb = S // Q_BLOCK
    bi = jnp.arange(B)[:, None, None]
    key_pos = jnp.arange(S)

    def block(args):
        qb, qib, wib, q0 = args
        qpos = q0 + jnp.arange(Q_BLOCK)
        sc = indexer_scores(qib, wib, ki)
        sc = jnp.where((key_pos[None, :] <= qpos[:, None])[None], sc, -jnp.inf)
        _, idx = lax.top_k(sc, topk)
        valid = idx <= qpos[None, :, None]
        return sparse_attend(qb, k[bi, idx], v[bi, idx], valid)

    def to_blocks(a):
        return a.reshape(B, nb, Q_BLOCK, *a.shape[2:]).swapaxes(0, 1)

    out = lax.map(block, (to_blocks(q), to_blocks(qi), to_blocks(wi), jnp.arange(nb) * Q_BLOCK))
    return out.swapaxes(0, 1).reshape(B, S, N_HEADS, HEAD_DIM)


def sample_sparse_attention(q, k_new, v_new, qi, ki_new, wi, cache_k, cache_v, cache_idx_k, page_table, layer):
    B, T = q.shape[:2]
    past_len = page_table.shape[1] * PAGE_SIZE
    L = past_len + T
    topk = min(TOPK_MAX, L // 4)
    ki_past = cache_idx_k[page_table, layer].reshape(B, past_len, IDX_DIM)
    ki_all = jnp.concatenate([ki_past, ki_new.astype(ki_past.dtype)], axis=1)
    qpos = past_len + jnp.arange(T)
    sc = indexer_scores(qi, wi, ki_all)
    sc = jnp.where((jnp.arange(L)[None, :] <= qpos[:, None])[None], sc, -jnp.inf)
    _, idx = lax.top_k(sc, topk)
    bi = jnp.arange(B)[:, None, None]
    in_past = (idx < past_len)[..., None, None]
    pp = jnp.minimum(idx, past_len - 1)
    phys = page_table[bi, pp // PAGE_SIZE]
    off = pp % PAGE_SIZE
    npos = jnp.clip(idx - past_len, 0, T - 1)
    k_sel = jnp.where(in_past, cache_k[phys, layer, off], k_new[bi, npos])
    v_sel = jnp.where(in_past, cache_v[phys, layer, off], v_new[bi, npos])
    valid = idx <= qpos[None, :, None]
    return sparse_attend(q, k_sel, v_sel, valid)


def merge_out(a_pool, a_attn, w_out):
    B, T = a_pool.shape[:2]
    return jnp.concatenate([a_pool, a_attn.reshape(B, T, ATTN_WIDTH)], axis=-1) @ w_out


def sq_relu_mlp(h, w1, w2):
    return jnp.square(jax.nn.relu(h @ w1)) @ w2


def setup_inputs(seed: int = 0) -> dict:
    key = jax.random.key(seed)
    ks = jax.random.split(key, 20)
    n_pages = PAST_LEN // PAGE_SIZE
    n_phys = (DEC_BATCH * n_pages * 5) // 4
    f32 = jnp.float32
    nrm = lambda k, shape: jax.random.normal(k, shape, dtype=f32)
    page_table = jax.random.permutation(ks[6], n_phys)[:DEC_BATCH * n_pages]
    page_table = page_table.reshape(DEC_BATCH, n_pages).astype(jnp.int32)
    return {
        'x_prompt': nrm(ks[0], (BATCH, SEQ, D_MODEL)),
        'x_sample': nrm(ks[1], (DEC_BATCH, DEC_SEQ, D_MODEL)),
        'cache_k': nrm(ks[2], (n_phys, DEPTH, PAGE_SIZE, N_KV_HEADS, HEAD_DIM)),
        'cache_v': nrm(ks[3], (n_phys, DEPTH, PAGE_SIZE, N_KV_HEADS, HEAD_DIM)),
        'cache_idx_k': nrm(ks[4], (n_phys, DEPTH, PAGE_SIZE, IDX_DIM)),
        'state_pool': nrm(ks[5], (DEPTH, DEC_BATCH, POOL_BUF, POOL_WIDTH)),
        'page_table': page_table,
        'w_in': nrm(ks[7], (DEPTH, D_MODEL, D_IN)) * D_MODEL ** -0.5,
        'w_out': nrm(ks[8], (DEPTH, D_MODEL, D_MODEL)) * D_MODEL ** -0.5,
        'pool_w': nrm(ks[9], (DEPTH, POOL_GROUPS, POOL_GROUP_DIM, POOL_GROUP_DIM)) * POOL_GROUP_DIM ** -0.5,
        'pool_scale': 1.0 + 0.1 * nrm(ks[10], (DEPTH, POOL_WIDTH)),
        'norm_mix': 1.0 + 0.05 * nrm(ks[11], (DEPTH, D_MODEL)),
        'norm_ffn': 1.0 + 0.05 * nrm(ks[12], (DEPTH, D_MODEL)),
        'w_ff1': nrm(ks[13], (DEPTH, D_MODEL, D_FF)) * D_MODEL ** -0.5,
        'w_ff2': nrm(ks[14], (DEPTH, D_FF, D_MODEL)) * D_FF ** -0.5,
        'norm_final': 1.0 + 0.05 * nrm(ks[15], (D_MODEL,)),
    }


def reference(x_prompt, x_sample, cache_k, cache_v, cache_idx_k, state_pool, page_table,
              w_in, w_out, pool_w, pool_scale, norm_mix, norm_ffn, w_ff1, w_ff2, norm_final):
    S = x_prompt.shape[1]
    T = x_sample.shape[1]
    past_len = page_table.shape[1] * PAGE_SIZE
    pos_p = jnp.arange(S)
    pos_s = past_len + jnp.arange(T)
    xp, xs = x_prompt, x_sample
    kp_l, vp_l, ip_l, pp_l = [], [], [], []
    ks_l, vs_l, is_l, ps_l = [], [], [], []
    for l in range(DEPTH):
        h = rmsnorm(xp, norm_mix[l])
        u, q, k, v, qi, ki, wi = project(h, w_in[l], pos_p)
        a_pool = pool_mix(u, 0, 0, pool_w[l], pool_scale[l])
        a_attn = prompt_sparse_attention(q, k, v, qi, ki, wi)
        xp = xp + merge_out(a_pool, a_attn, w_out[l])
        xp = xp + sq_relu_mlp(rmsnorm(xp, norm_ffn[l]), w_ff1[l], w_ff2[l])
        kp_l.append(k)
        vp_l.append(v)
        ip_l.append(ki)
        pp_l.append(u[:, -POOL_BUF:])
        h = rmsnorm(xs, norm_mix[l])
        u, q, k, v, qi, ki, wi = project(h, w_in[l], pos_s)
        u_ext = jnp.concatenate([state_pool[l].astype(u.dtype), u], axis=1)
        a_pool = pool_mix(u_ext, POOL_BUF, past_len, pool_w[l], pool_scale[l])
        a_attn = sample_sparse_attention(q, k, v, qi, ki, wi, cache_k, cache_v, cache_idx_k, page_table, l)
        xs = xs + merge_out(a_pool, a_attn, w_out[l])
        xs = xs + sq_relu_mlp(rmsnorm(xs, norm_ffn[l]), w_ff1[l], w_ff2[l])
        ks_l.append(k)
        vs_l.append(v)
        is_l.append(ki)
        ps_l.append(u_ext[:, -POOL_BUF:])
    y_prompt = rmsnorm(xp, norm_final)
    y_sample = rmsnorm(xs, norm_final)
    return (y_prompt, y_sample,
            jnp.stack(kp_l, axis=1), jnp.stack(vp_l, axis=1), jnp.stack(ip_l, axis=1), jnp.stack(pp_l, axis=0),
            jnp.stack(ks_l, axis=1), jnp.stack(vs_l, axis=1), jnp.stack(is_l, axis=1), jnp.stack(ps_l, axis=0))
```

```python
import functools

import jax
import jax.numpy as jnp
import numpy as np
from jax import lax
from jax.experimental import pallas as pl
from jax.experimental.pallas import tpu as pltpu

F32 = jnp.float32
BF16 = jnp.bfloat16
I32 = jnp.int32

POOL_WINDOWS = (2, 4, 8, 16)
POOL_GROUPS = len(POOL_WINDOWS)
POOL_HALO = 16
HEAD_DIM = 64
ROT_DIM = 16
IDX_HEADS = 4
IDX_DIM = 64
TOPK_MAX = 256
ROPE_THETA = 500000.0
EPS = 1e-6
PAGE = 128
Q_BLOCK = 128

LANES = 128
SUBLANES = 8
VMEM_LIMIT = 56 * 1024 * 1024

INT_MIN = -2147483648
INT_MAX = 2147483647
NEG = -1e30


def _iota(shape, dim):
    return lax.broadcasted_iota(I32, shape, dim)


def _rms(x, g):
    r = lax.rsqrt(jnp.mean(x * x, axis=-1, keepdims=True) + EPS)
    return (x * r) * g


def _score_keys(sc):
    b = lax.bitcast_convert_type(sc, I32)
    m = b >> 31
    return (b ^ (m & INT_MAX)) - m


def _rope(seg, cos, sa, sb):
    outs = []
    for c in range(seg.shape[1] // LANES):
        s = seg[:, c * LANES:(c + 1) * LANES]
        outs.append(s * cos + pltpu.roll(s, LANES - ROT_DIM // 2, 1) * sa + pltpu.roll(s, ROT_DIM // 2, 1) * sb)
    return outs[0] if len(outs) == 1 else jnp.concatenate(outs, axis=1)


def _proj_kernel(x_ref, g_ref, w_ref, cos_ref, sa_ref, sb_ref, *out_refs, sizes, emit_vt):
    if emit_vt:
        u_ref, q_ref, k_ref, v_ref, kb_ref, vb_ref, qi_ref, ki_ref, kib_ref, wi_ref, vt_ref = out_refs
    else:
        u_ref, q_ref, k_ref, v_ref, kb_ref, vb_ref, qi_ref, ki_ref, kib_ref, wi_ref = out_refs
    n_u, n_q, n_k, n_v, n_qi = sizes
    h = _rms(x_ref[...], g_ref[...]).astype(BF16)
    cos, sa, sb = cos_ref[...], sa_ref[...], sb_ref[...]

    def seg(a, n):
        return jnp.dot(h, w_ref[:, a:a + n], preferred_element_type=F32)

    o = 0
    u_ref[...] = seg(o, n_u)
    o += n_u
    q_ref[...] = (_rope(seg(o, n_q), cos, sa, sb) * (HEAD_DIM ** -0.5)).astype(BF16)
    o += n_q
    k = _rope(seg(o, n_k), cos, sa, sb)
    k_ref[...] = k
    kb_ref[...] = k.astype(BF16)
    o += n_k
    v = seg(o, n_v)
    v_ref[...] = v
    vb_ref[...] = v.astype(BF16)
    if emit_vt:
        vt_ref[0] = v.T.astype(BF16)
    o += n_v
    qi_ref[...] = _rope(seg(o, n_qi), cos, sa, sb).astype(BF16)
    o += n_qi
    tail = seg(o, LANES)
    ki = _rope(tail, cos, sa, sb)[:, :IDX_DIM]
    ki_ref[...] = ki
    kib_ref[...] = ki.astype(BF16)
    wi_ref[...] = pltpu.roll(tail, LANES - IDX_DIM, 1) * (IDX_HEADS ** -0.5 * IDX_DIM ** -0.5)


def _proj(x, g, w, tabs, *, tm, tab_blocks, sizes, emit_vt):
    n, d = x.shape
    n_u, n_q, n_k, n_v, n_qi = sizes
    row = lambda i: (i, 0)
    tab = lambda i: (i % tab_blocks, 0)
    outs = [(n_u, F32), (n_q, BF16), (n_k, F32), (n_v, F32), (n_k, BF16), (n_v, BF16), (n_qi, BF16),
            (IDX_DIM, F32), (IDX_DIM, BF16), (LANES, F32)]
    out_shape = [jax.ShapeDtypeStruct((n, c), t) for c, t in outs]
    out_specs = [pl.BlockSpec((tm, c), row) for c, _ in outs]
    if emit_vt:
        out_shape.append(jax.ShapeDtypeStruct((n // tm, n_v, tm), BF16))
        out_specs.append(pl.BlockSpec((1, n_v, tm), lambda i: (i, 0, 0)))
    return pl.pallas_call(
        functools.partial(_proj_kernel, sizes=sizes, emit_vt=emit_vt),
        grid=(n // tm,),
        in_specs=[pl.BlockSpec((tm, d), row), pl.BlockSpec((1, d), lambda i: (0, 0)),
                  pl.BlockSpec(w.shape, lambda i: (0, 0)),
                  pl.BlockSpec((tm, LANES), tab), pl.BlockSpec((tm, LANES), tab), pl.BlockSpec((tm, LANES), tab)],
        out_specs=out_specs, out_shape=out_shape,
        compiler_params=pltpu.CompilerParams(dimension_semantics=("arbitrary",), vmem_limit_bytes=VMEM_LIMIT),
        name="proj",
    )(x, g, w, *tabs)


def _rope_tables(pos):
    half = ROT_DIM // 2
    inv = jnp.power(jnp.float32(ROPE_THETA), -jnp.arange(half, dtype=F32) * 2.0 / ROT_DIM)
    ang = pos.astype(F32)[:, None] * inv[None, :]
    cos, sin = jnp.cos(ang), jnp.sin(ang)
    t = pos.shape[0]
    one = jnp.ones((t, HEAD_DIM - ROT_DIM), F32)
    zero = jnp.zeros((t, HEAD_DIM - ROT_DIM), F32)
    zh = jnp.zeros((t, half), F32)
    reps = LANES // HEAD_DIM
    c = jnp.tile(jnp.concatenate([cos, cos, one], axis=1), (1, reps))
    sa = jnp.tile(jnp.concatenate([-sin, zh, zero], axis=1), (1, reps))
    sb = jnp.tile(jnp.concatenate([zh, sin, zero], axis=1), (1, reps))
    return c, sa, sb


def _pool_kernel(u_ref, halo_ref, pw_ref, ps_ref, o_ref, *, tm, pos0, zero_first):
    i = pl.program_id(1)
    u = u_ref[0]
    halo = halo_ref[0]
    if zero_first:
        halo = jnp.where(i == 0, 0.0, halo)
    ext = jnp.concatenate([halo, u], axis=0)
    t = pos0 + i * tm + _iota((tm, LANES), 0)
    outs = []
    for g, win in enumerate(POOL_WINDOWS):
        s = ext[:, g * LANES:(g + 1) * LANES]
        sh = 1
        while sh < win:
            s = s + pltpu.roll(s, sh, 0)
            sh *= 2
        cnt = jnp.minimum(win, t + 1).astype(F32)
        d = s[POOL_HALO:] / cnt - u[:, g * LANES:(g + 1) * LANES]
        y = jnp.dot(d.astype(BF16), pw_ref[g], preferred_element_type=F32)
        outs.append(y * ps_ref[:, g * LANES:(g + 1) * LANES])
    o_ref[0] = jnp.concatenate(outs, axis=1).astype(BF16)


def _pool(u3, halo3, pw, ps, *, tm, pos0, zero_first):
    b, t, width = u3.shape
    assert width == POOL_GROUPS * LANES and tm % SUBLANES == 0
    if zero_first:
        hb = tm // POOL_HALO
        halo_map = lambda bi, i: (bi, jnp.maximum(i * hb - 1, 0), 0)
    else:
        assert t == tm
        halo_map = lambda bi, i: (bi, 0, 0)
    return pl.pallas_call(
        functools.partial(_pool_kernel, tm=tm, pos0=pos0, zero_first=zero_first),
        grid=(b, t // tm),
        in_specs=[pl.BlockSpec((1, tm, width), lambda bi, i: (bi, i, 0)),
                  pl.BlockSpec((1, POOL_HALO, width), halo_map),
                  pl.BlockSpec(pw.shape, lambda bi, i: (0, 0, 0)),
                  pl.BlockSpec((1, width), lambda bi, i: (0, 0))],
        out_specs=pl.BlockSpec((1, tm, width), lambda bi, i: (bi, i, 0)),
        out_shape=jax.ShapeDtypeStruct((b, t, width), BF16),
        compiler_params=pltpu.CompilerParams(dimension_semantics=("arbitrary", "arbitrary")),
        name="pool",
    )(u3, halo3, pw, ps)


def _mlp_kernel(x_ref, ap_ref, aa_ref, wo_ref, g_ref, w1_ref, w2_ref, gf_ref, *out_refs, ff_chunk, final):
    n_p = ap_ref.shape[1]
    x1 = (x_ref[...]
          + jnp.dot(ap_ref[...], wo_ref[:n_p, :], preferred_element_type=F32)
          + jnp.dot(aa_ref[...], wo_ref[n_p:, :], preferred_element_type=F32))
    h = _rms(x1, g_ref[...]).astype(BF16)
    ff = None
    for c in range(w1_ref.shape[1] // ff_chunk):
        a = jnp.dot(h, w1_ref[:, c * ff_chunk:(c + 1) * ff_chunk], preferred_element_type=F32)
        a = jnp.square(jnp.maximum(a, 0.0)).astype(BF16)
        part = jnp.dot(a, w2_ref[c * ff_chunk:(c + 1) * ff_chunk, :], preferred_element_type=F32)
        ff = part if ff is None else ff + part
    x2 = ff + x1
    out_refs[0][...] = x2
    if final:
        out_refs[1][...] = _rms(x2, gf_ref[...])


def _mlp(x, ap, aa, wo, g, w1, w2, gf, *, tm, final):
    n, d = x.shape
    row = lambda i: (i, 0)
    full = lambda i: (0, 0)
    n_out = 2 if final else 1
    return pl.pallas_call(
        functools.partial(_mlp_kernel, ff_chunk=1024, final=final),
        grid=(n // tm,),
        in_specs=[pl.BlockSpec((tm, d), row), pl.BlockSpec((tm, ap.shape[1]), row), pl.BlockSpec((tm, aa.shape[1]), row),
                  pl.BlockSpec(wo.shape, full), pl.BlockSpec((1, d), full),
                  pl.BlockSpec(w1.shape, full), pl.BlockSpec(w2.shape, full), pl.BlockSpec((1, d), full)],
        out_specs=[pl.BlockSpec((tm, d), row)] * n_out,
        out_shape=[jax.ShapeDtypeStruct((n, d), F32)] * n_out,
        compiler_params=pltpu.CompilerParams(dimension_semantics=("arbitrary",), vmem_limit_bytes=VMEM_LIMIT),
        name="mlp",
    )(x, ap, aa, wo, g, w1, w2, gf)


def _attn_prompt_kernel(q_ref, qi_ref, wi_ref, ki_ref, k_ref, vt_ref, a_ref, keys_ref, acc_ref, wp_ref, x_ref,
                        *, kc, topk, nbits):
    j = pl.program_id(1)
    n_heads = q_ref.shape[1] // HEAD_DIM
    n_pairs = n_heads // 2
    nch = (j * Q_BLOCK) // kc + 1
    qpos = j * Q_BLOCK + _iota((1, LANES), 1)

    qi_t = [qi_ref[:, c * LANES:(c + 1) * LANES].astype(F32).T for c in range(IDX_HEADS * IDX_DIM // LANES)]
    per = LANES // IDX_DIM
    qi_w = jnp.concatenate(
        [qi_t[h // per][(h % per) * IDX_DIM:(h % per + 1) * IDX_DIM, :] for h in range(IDX_HEADS)],
        axis=1).astype(BF16)
    w_t = wi_ref[...].T
    zeros = jnp.zeros((HEAD_DIM, LANES), F32)
    for p in range(n_pairs):
        q_t = q_ref[:, p * LANES:(p + 1) * LANES].astype(F32).T
        top = jnp.concatenate([q_t[:HEAD_DIM], zeros], axis=1)
        bot = jnp.concatenate([zeros, q_t[HEAD_DIM:]], axis=1)
        wp_ref[p] = jnp.concatenate([top, bot], axis=0).astype(BF16)

    def score_chunk(c, carry):
        r0 = pl.multiple_of(c * kc, kc)
        d = jnp.dot(ki_ref[pl.ds(r0, kc), :], qi_w, preferred_element_type=F32)
        sc = w_t[0:1, :] * jnp.maximum(d[:, :LANES], 0.0)
        for h in range(1, IDX_HEADS):
            sc = sc + w_t[h:h + 1, :] * jnp.maximum(d[:, h * LANES:(h + 1) * LANES], 0.0)
        kpos = r0 + _iota((kc, LANES), 0)
        keys_ref[pl.ds(r0, kc), :] = jnp.where(kpos <= qpos, _score_keys(sc), INT_MIN)
        return carry

    lax.fori_loop(0, nch, score_chunk, 0)

    def count(pred):
        def body(c, acc):
            r0 = pl.multiple_of(c * kc, kc)
            hit = jnp.where(pred(keys_ref[pl.ds(r0, kc), :], r0), 1.0, 0.0)
            return acc + jnp.sum(hit.reshape(kc // SUBLANES, SUBLANES, LANES), axis=0)
        acc = lax.fori_loop(0, nch, body, jnp.zeros((SUBLANES, LANES), F32))
        return jnp.sum(acc, axis=0, keepdims=True)

    def count_ge(cand):
        return count(lambda blk, r0: blk >= cand)

    kf = float(topk)
    t = jnp.where(count_ge(jnp.zeros((1, LANES), I32)) >= kf, 0, INT_MIN).astype(I32)

    def bit_step(i, t):
        cand = t | lax.shift_left(jnp.int32(1), 30 - i)
        return jnp.where(count_ge(cand) >= kf, cand, t)

    t = lax.fori_loop(0, 31, bit_step, t)

    c_gt = count_ge(t + 1)
    c_ge = count_ge(t)
    need = kf - c_gt
    full_rows = t > INT_MIN
    has_tie = jnp.logical_and(c_ge - c_gt > need, full_rows)
    x_ref[...] = jnp.broadcast_to(jnp.where(full_rows, INT_MAX, -1).astype(I32), x_ref.shape)

    @pl.when(jnp.max(jnp.where(has_tie, 1.0, 0.0)) > 0.0)
    def _():
        def pos_step(i, x):
            cand = x | lax.shift_left(jnp.int32(1), nbits - 1 - i)
            g = count(lambda blk, r0: jnp.logical_and(blk == t, r0 + _iota((kc, LANES), 0) < cand))
            return jnp.where(g < need, cand, x)
        xs = lax.fori_loop(0, nbits, pos_step, jnp.zeros((1, LANES), I32))
        x_ref[...] = jnp.broadcast_to(jnp.where(has_tie, xs, x_ref[0:1, :]), x_ref.shape)

    x = x_ref[0:1, :]

    acc_ref[...] = jnp.zeros_like(acc_ref)

    def attend_chunk(c, carry):
        ms, ls = carry
        r0 = pl.multiple_of(c * kc, kc)
        blk = keys_ref[pl.ds(r0, kc), :]
        kpos = r0 + _iota((kc, LANES), 0)
        sel = jnp.logical_or(blk > t, jnp.logical_and(blk == t, kpos <= x))
        bias = jnp.where(sel, 0.0, NEG)
        ms_new, ls_new = [], []
        for p in range(n_pairs):
            s_pair = jnp.dot(k_ref[pl.ds(r0, kc), p * LANES:(p + 1) * LANES], wp_ref[p],
                             preferred_element_type=F32)
            ps, alphas = [], []
            for hh in range(2):
                h = 2 * p + hh
                s = s_pair[:, hh * LANES:(hh + 1) * LANES] + bias
                m_new = jnp.maximum(ms[h], jnp.max(s, axis=0, keepdims=True))
                alpha = jnp.exp(ms[h] - m_new)
                pr = jnp.exp(s - m_new)
                ls_new.append(alpha * ls[h] + jnp.sum(pr, axis=0, keepdims=True))
                ms_new.append(m_new)
                ps.append(pr.astype(BF16))
                alphas.append(alpha)
            o = jnp.dot(vt_ref[c, p * LANES:(p + 1) * LANES, :], jnp.concatenate(ps, axis=1),
                        preferred_element_type=F32)
            for hh in range(2):
                rows = pl.ds((2 * p + hh) * HEAD_DIM, HEAD_DIM)
                acc_ref[rows, :] = (alphas[hh] * acc_ref[rows, :]
                                    + o[hh * HEAD_DIM:(hh + 1) * HEAD_DIM, hh * LANES:(hh + 1) * LANES])
        return tuple(ms_new), tuple(ls_new)

    init = (tuple(jnp.full((1, LANES), NEG, F32) for _ in range(n_heads)),
            tuple(jnp.zeros((1, LANES), F32) for _ in range(n_heads)))
    _, ls = lax.fori_loop(0, nch, attend_chunk, init)

    for p in range(n_pairs):
        o_t = jnp.concatenate(
            [acc_ref[h * HEAD_DIM:(h + 1) * HEAD_DIM, :] / ls[h] for h in (2 * p, 2 * p + 1)], axis=0)
        a_ref[:, p * LANES:(p + 1) * LANES] = o_t.T.astype(BF16)


def _attn_prompt(q, qi, wi, kib, kb, vt, *, b, s, kc):
    n = q.shape[0]
    width = q.shape[1]
    assert n == b * s and s % kc == 0 and kc % Q_BLOCK == 0 and width % (2 * HEAD_DIM) == 0
    nq = s // Q_BLOCK
    nck = s // kc
    topk = min(TOPK_MAX, s // 4)
    nbits = max(1, int(np.ceil(np.log2(s))))
    qrow = lambda bi, j: (bi * nq + j, 0)
    resident = pl.Buffered(1)
    return pl.pallas_call(
        functools.partial(_attn_prompt_kernel, kc=kc, topk=topk, nbits=nbits),
        grid=(b, nq),
        in_specs=[pl.BlockSpec((Q_BLOCK, width), qrow),
                  pl.BlockSpec((Q_BLOCK, qi.shape[1]), qrow),
                  pl.BlockSpec((Q_BLOCK, LANES), qrow),
                  pl.BlockSpec((s, IDX_DIM), lambda bi, j: (bi, 0), pipeline_mode=resident),
                  pl.BlockSpec((s, width), lambda bi, j: (bi, 0), pipeline_mode=resident),
                  pl.BlockSpec((nck, width, kc), lambda bi, j: (bi, 0, 0), pipeline_mode=resident)],
        out_specs=pl.BlockSpec((Q_BLOCK, width), qrow),
        out_shape=jax.ShapeDtypeStruct((n, width), BF16),
        scratch_shapes=[pltpu.VMEM((s, LANES), I32),
                        pltpu.VMEM((width, LANES), F32),
                        pltpu.VMEM((width // LANES, LANES, 2 * LANES), BF16),
                        pltpu.VMEM((SUBLANES, LANES), I32)],
        compiler_params=pltpu.CompilerParams(dimension_semantics=("arbitrary", "arbitrary"),
                                             vmem_limit_bytes=VMEM_LIMIT),
        name="attn_prompt",
    )(q, qi, wi, kib, kb, vt)


def _attn_sample_kernel(pt_ref, q_ref, qi_ref, wi_ref, kn_ref, vn_ref, kin_ref, ck_ref, cv_ref, ci_ref,
                        a_ref, kibuf, keys_ref, keysn_ref, kbuf, vbuf, acc_ref, x_ref, sem_i, sem_kv,
                        *, layer, n_pages, gp, topk, nbits, past_len):
    b = pl.program_id(0)
    t_new = q_ref.shape[1]
    width = q_ref.shape[2]
    n_heads = width // HEAD_DIM
    ng = n_pages // gp
    gl = gp * PAGE

    def ki_copy(p):
        return pltpu.make_async_copy(ci_ref.at[pt_ref[b, p], layer], kibuf.at[p], sem_i.at[0])

    def kv_copies(g, slot):
        cps = []
        for i in range(gp):
            page = pt_ref[b, g * gp + i]
            cps.append(pltpu.make_async_copy(ck_ref.at[page, layer], kbuf.at[slot, i], sem_kv.at[0, slot]))
            cps.append(pltpu.make_async_copy(cv_ref.at[page, layer], vbuf.at[slot, i], sem_kv.at[1, slot]))
        return cps

    def start_ki(p, c):
        ki_copy(p).start()
        return c

    def wait_ki(p, c):
        ki_copy(p).wait()
        return c

    lax.fori_loop(0, n_pages, start_ki, 0)
    for cp in kv_copies(0, 0):
        cp.start()

    qi = qi_ref[0].astype(F32)
    qi_rows = jnp.concatenate([qi[:, h * IDX_DIM:(h + 1) * IDX_DIM] for h in range(IDX_HEADS)], axis=0).astype(BF16)
    wi = wi_ref[0]
    w_col = jnp.concatenate([wi[:, h:h + 1] for h in range(IDX_HEADS)], axis=0)
    nt = (((1,), (1,)), ((), ()))

    def idx_scores(kmat):
        d = lax.dot_general(qi_rows, kmat, nt, preferred_element_type=F32)
        sc = w_col[0:t_new] * jnp.maximum(d[0:t_new], 0.0)
        for h in range(1, IDX_HEADS):
            sc = sc + w_col[h * t_new:(h + 1) * t_new] * jnp.maximum(d[h * t_new:(h + 1) * t_new], 0.0)
        return sc

    lax.fori_loop(0, n_pages, wait_ki, 0)

    def score_group(g, c):
        kmat = kibuf[pl.ds(g * gp, gp)].reshape(gl, IDX_DIM).astype(BF16)
        keys_ref[g] = _score_keys(idx_scores(kmat))
        return c

    lax.fori_loop(0, ng, score_group, 0)
    pad = jnp.zeros((LANES - t_new, IDX_DIM), BF16)
    sc_new = idx_scores(jnp.concatenate([kin_ref[0], pad], axis=0))
    new_ok = _iota((t_new, LANES), 1) <= _iota((t_new, LANES), 0)
    keysn_ref[...] = jnp.where(new_ok, _score_keys(sc_new), INT_MIN)

    def count(pred_past, pred_new):
        acc = jnp.zeros((t_new, gl), F32)
        for g in range(ng):
            acc = acc + jnp.where(pred_past(keys_ref[g], g), 1.0, 0.0)
        tot = jnp.sum(acc, axis=1, keepdims=True)
        return tot + jnp.sum(jnp.where(pred_new(keysn_ref[...]), 1.0, 0.0), axis=1, keepdims=True)

    def count_ge(cand):
        return count(lambda blk, g: blk >= cand, lambda blk: blk >= cand)

    kf = float(topk)
    t = jnp.where(count_ge(jnp.zeros((t_new, 1), I32)) >= kf, 0, INT_MIN).astype(I32)

    def bit_step(i, t):
        cand = t | lax.shift_left(jnp.int32(1), 30 - i)
        return jnp.where(count_ge(cand) >= kf, cand, t)

    t = lax.fori_loop(0, 31, bit_step, t)

    c_gt = count_ge(t + 1)
    c_ge = count_ge(t)
    need = kf - c_gt
    full_rows = t > INT_MIN
    has_tie = jnp.logical_and(c_ge - c_gt > need, full_rows)
    x_ref[...] = jnp.broadcast_to(jnp.where(full_rows, INT_MAX, -1).astype(I32), x_ref.shape)
    lane_g = _iota((t_new, gl), 1)
    lane_n = past_len + _iota((t_new, LANES), 1)

    @pl.when(jnp.max(jnp.where(has_tie, 1.0, 0.0)) > 0.0)
    def _():
        def pos_step(i, x):
            cand = x | lax.shift_left(jnp.int32(1), nbits - 1 - i)
            g = count(lambda blk, g: jnp.logical_and(blk == t, g * gl + lane_g < cand),
                      lambda blk: jnp.logical_and(blk == t, lane_n < cand))
            return jnp.where(g < need, cand, x)
        xs = lax.fori_loop(0, nbits, pos_step, jnp.zeros((t_new, 1), I32))
        x_ref[...] = jnp.broadcast_to(jnp.where(has_tie, xs, x_ref[:, 0:1]), x_ref.shape)

    x = x_ref[:, 0:1]

    rows = n_heads * t_new
    q = q_ref[0].astype(F32)
    own = _iota((rows, width), 0) // t_new == _iota((rows, width), 1) // HEAD_DIM
    qbd = jnp.where(own, jnp.concatenate([q] * n_heads, axis=0), 0.0).astype(BF16)

    def attend(kmat, vmat, blk, kpos, carry):
        m, l = carry
        sel = jnp.logical_or(blk > t, jnp.logical_and(blk == t, kpos <= x))
        bias = jnp.where(sel, 0.0, NEG)
        s = lax.dot_general(qbd, kmat, nt, preferred_element_type=F32) + jnp.concatenate([bias] * n_heads, axis=0)
        m_new = jnp.maximum(m, jnp.max(s, axis=1, keepdims=True))
        alpha = jnp.exp(m - m_new)
        pr = jnp.exp(s - m_new)
        l_new = alpha * l + jnp.sum(pr, axis=1, keepdims=True)
        acc_ref[...] = alpha * acc_ref[...] + jnp.dot(pr.astype(BF16), vmat, preferred_element_type=F32)
        return m_new, l_new

    acc_ref[...] = jnp.zeros_like(acc_ref)

    def group_step(g, carry):
        slot = g % 2

        @pl.when(g + 1 < ng)
        def _():
            for cp in kv_copies(g + 1, 1 - slot):
                cp.start()

        for cp in kv_copies(g, slot):
            cp.wait()
        kmat = kbuf[slot].reshape(gl, width).astype(BF16)
        vmat = vbuf[slot].reshape(gl, width).astype(BF16)
        return attend(kmat, vmat, keys_ref[g], g * gl + lane_g, carry)

    carry = (jnp.full((rows, 1), NEG, F32), jnp.zeros((rows, 1), F32))
    carry = lax.fori_loop(0, ng, group_step, carry)
    padw = jnp.zeros((LANES - t_new, width), BF16)
    _, l = attend(jnp.concatenate([kn_ref[0], padw], axis=0), jnp.concatenate([vn_ref[0], padw], axis=0),
                  keysn_ref[...], lane_n, carry)

    o = jnp.where(own, acc_ref[...] / l, 0.0)
    out = o[0:t_new]
    for h in range(1, n_heads):
        out = out + o[h * t_new:(h + 1) * t_new]
    a_ref[0] = out.astype(BF16)


def _attn_sample(page_table, q, qi, wi, kn, vn, kin, ck, cv, ci, *, layer, gp):
    nb, t_new, width = q.shape
    n_pages = page_table.shape[1]
    assert t_new == SUBLANES and n_pages % gp == 0 and ck.shape[2] == PAGE
    past_len = n_pages * PAGE
    topk = min(TOPK_MAX, (past_len + t_new) // 4)
    nbits = max(1, int(np.ceil(np.log2(past_len + LANES))))
    ng = n_pages // gp
    gl = gp * PAGE
    rows = (width // HEAD_DIM) * t_new
    per_b = lambda c: pl.BlockSpec((1, t_new, c), lambda bi, pt: (bi, 0, 0))
    any_spec = pl.BlockSpec(memory_space=pl.ANY)
    grid_spec = pltpu.PrefetchScalarGridSpec(
        num_scalar_prefetch=1,
        grid=(nb,),
        in_specs=[per_b(width), per_b(qi.shape[2]), per_b(LANES), per_b(width), per_b(width), per_b(IDX_DIM),
                  any_spec, any_spec, any_spec],
        out_specs=per_b(width),
        scratch_shapes=[pltpu.VMEM((n_pages, PAGE, IDX_DIM), F32),
                        pltpu.VMEM((ng, t_new, gl), I32),
                        pltpu.VMEM((t_new, LANES), I32),
                        pltpu.VMEM((2, gp, PAGE, width), F32),
                        pltpu.VMEM((2, gp, PAGE, width), F32),
                        pltpu.VMEM((rows, width), F32),
                        pltpu.VMEM((t_new, LANES), I32),
                        pltpu.SemaphoreType.DMA((1,)),
                        pltpu.SemaphoreType.DMA((2, 2))])
    return pl.pallas_call(
        functools.partial(_attn_sample_kernel, layer=layer, n_pages=n_pages, gp=gp, topk=topk, nbits=nbits,
                          past_len=past_len),
        grid_spec=grid_spec,
        out_shape=jax.ShapeDtypeStruct((nb, t_new, width), BF16),
        compiler_params=pltpu.CompilerParams(dimension_semantics=("arbitrary",), vmem_limit_bytes=VMEM_LIMIT),
        name="attn_sample",
    )(page_table, q, qi, wi, kn, vn, kin, ck, cv, ci)


def kernel(x_prompt, x_sample, cache_k, cache_v, cache_idx_k, state_pool, page_table, w_in, w_out, pool_w,
           pool_scale, norm_mix, norm_ffn, w_ff1, w_ff2, norm_final):
    b, s, d = x_prompt.shape
    nb, t_new, _ = x_sample.shape
    depth = w_in.shape[0]
    n_pool = pool_scale.shape[1]
    n_kv_heads, head_dim = cache_k.shape[3], cache_k.shape[4]
    n_k = n_kv_heads * head_dim
    n_q = w_out.shape[1] - n_pool
    sizes = (n_pool, n_q, n_k, n_k, IDX_HEADS * IDX_DIM)
    d_in = w_in.shape[2]
    assert head_dim == HEAD_DIM and d_in == sum(sizes) + IDX_DIM + IDX_HEADS
    past_len = page_table.shape[1] * PAGE
    kc = min(512, s)

    d_pad = sum(sizes) + LANES
    w_in_b = jnp.pad(w_in, ((0, 0), (0, 0), (0, d_pad - d_in))).astype(BF16)
    w_out_b, w1_b, w2_b, pw_b = (a.astype(BF16) for a in (w_out, w_ff1, w_ff2, pool_w))
    tabs_p = _rope_tables(jnp.arange(s))
    tabs_s = tuple(jnp.tile(a, (nb, 1)) for a in _rope_tables(past_len + jnp.arange(t_new)))
    ck = cache_k.reshape(cache_k.shape[0], depth, PAGE, n_k)
    cv = cache_v.reshape(cache_v.shape[0], depth, PAGE, n_k)
    state_halo = jnp.pad(state_pool, ((0, 0), (0, 0), (POOL_HALO - state_pool.shape[2], 0), (0, 0)))
    gf = norm_final.reshape(1, d)

    xp = x_prompt.reshape(b * s, d)
    xs = x_sample.reshape(nb * t_new, d)
    kp_l, vp_l, ip_l, pp_l, ks_l, vs_l, is_l, ps_l = ([] for _ in range(8))
    yp = ys = None
    for l in range(depth):
        final = l == depth - 1
        g_mix, g_ffn = norm_mix[l].reshape(1, d), norm_ffn[l].reshape(1, d)
        scale = pool_scale[l].reshape(1, n_pool)

        u, q, k, v, kb, _, qi, ki, kib, wi, vt = _proj(xp, g_mix, w_in_b[l], tabs_p, tm=kc, tab_blocks=s // kc,
                                                        sizes=sizes, emit_vt=True)
        u3 = u.reshape(b, s, n_pool)
        a_pool = _pool(u3, u3, pw_b[l], scale, tm=kc, pos0=0, zero_first=True).reshape(b * s, n_pool)
        a_attn = _attn_prompt(q, qi, wi, kib, kb, vt, b=b, s=s, kc=kc)
        res = _mlp(xp, a_pool, a_attn, w_out_b[l], g_ffn, w1_b[l], w2_b[l], gf, tm=kc, final=final)
        xp = res[0]
        if final:
            yp = res[1]
        kp_l.append(k.reshape(b, s, n_kv_heads, head_dim))
        vp_l.append(v.reshape(b, s, n_kv_heads, head_dim))
        ip_l.append(ki.reshape(b, s, IDX_DIM))
        pp_l.append(u3[:, s - state_pool.shape[2]:])

        n_s = nb * t_new
        u, q, k, v, kb, vb, qi, ki, kib, wi = _proj(xs, g_mix, w_in_b[l], tabs_s, tm=n_s, tab_blocks=1,
                                                     sizes=sizes, emit_vt=False)
        u3 = u.reshape(nb, t_new, n_pool)
        a_pool = _pool(u3, state_halo[l], pw_b[l], scale, tm=t_new, pos0=past_len, zero_first=False)
        r3 = lambda a: a.reshape(nb, t_new, a.shape[1])
        a_attn = _attn_sample(page_table, r3(q), r3(qi), r3(wi), r3(kb), r3(vb), r3(kib), ck, cv, cache_idx_k,
                              layer=l, gp=min(8, page_table.shape[1]))
        res = _mlp(xs, a_pool.reshape(n_s, n_pool), a_attn.reshape(n_s, n_q), w_out_b[l], g_ffn, w1_b[l], w2_b[l],
                   gf, tm=n_s, final=final)
        xs = res[0]
        if final:
            ys = res[1]
        ks_l.append(k.reshape(nb, t_new, n_kv_heads, head_dim))
        vs_l.append(v.reshape(nb, t_new, n_kv_heads, head_dim))
        is_l.append(ki.reshape(nb, t_new, IDX_DIM))
        ps_l.append(jnp.concatenate([state_pool[l], u3], axis=1)[:, t_new:])

    return (yp.reshape(b, s, d), ys.reshape(nb, t_new, d),
            jnp.stack(kp_l, axis=1), jnp.stack(vp_l, axis=1), jnp.stack(ip_l, axis=1), jnp.stack(pp_l, axis=0),
            jnp.stack(ks_l, axis=1), jnp.stack(vs_l, axis=1), jnp.stack(is_l, axis=1), jnp.stack(ps_l, axis=0))
```

```python
import functools

import jax
import jax.numpy as jnp
import numpy as np
from jax import lax
from jax.experimental import pallas as pl
from jax.experimental.pallas import tpu as pltpu

F32 = jnp.float32
BF16 = jnp.bfloat16
I32 = jnp.int32

POOL_WINDOWS = (2, 4, 8, 16)
POOL_GROUPS = len(POOL_WINDOWS)
POOL_HALO = 16
HEAD_DIM = 64
ROT_DIM = 16
IDX_HEADS = 4
IDX_DIM = 64
TOPK_MAX = 256
ROPE_THETA = 500000.0
EPS = 1e-6
PAGE = 128
Q_BLOCK = 128
SUB_KEYS = 128
SEARCH_GROUP = 4

LANES = 128
SUBLANES = 8
VMEM_LIMIT = 56 * 1024 * 1024

BF16_ROWS = 16
VT_ROWS = LANES + BF16_ROWS
Q_SCALE = HEAD_DIM ** -0.5 * float(np.log2(np.e))

INT_MIN = -2147483648
INT_MAX = 2147483647
NEG = -1e30


def _iota(shape, dim):
    return lax.broadcasted_iota(I32, shape, dim)


def _rms(x, g):
    r = lax.rsqrt(jnp.mean(x * x, axis=-1, keepdims=True) + EPS)
    return (x * r) * g


def _score_keys(sc):
    b = lax.bitcast_convert_type(sc, I32)
    m = b >> 31
    return (b ^ (m & INT_MAX)) - m


def _rope(seg, cos, sa, sb):
    outs = []
    for c in range(seg.shape[1] // LANES):
        s = seg[:, c * LANES:(c + 1) * LANES]
        outs.append(s * cos + pltpu.roll(s, LANES - ROT_DIM // 2, 1) * sa + pltpu.roll(s, ROT_DIM // 2, 1) * sb)
    return outs[0] if len(outs) == 1 else jnp.concatenate(outs, axis=1)


def _proj_kernel(x_ref, g_ref, w_ref, cos_ref, sa_ref, sb_ref, *out_refs, sizes, emit_vt):
    if emit_vt:
        u_ref, q_ref, k_ref, v_ref, kb_ref, vb_ref, qi_ref, ki_ref, kib_ref, wi_ref, vt_ref = out_refs
    else:
        u_ref, q_ref, k_ref, v_ref, kb_ref, vb_ref, qi_ref, ki_ref, kib_ref, wi_ref = out_refs
    n_u, n_q, n_k, n_v, n_qi = sizes
    h = _rms(x_ref[...], g_ref[...]).astype(BF16)
    cos, sa, sb = cos_ref[...], sa_ref[...], sb_ref[...]

    def seg(a, n):
        return jnp.dot(h, w_ref[:, a:a + n], preferred_element_type=F32)

    o = 0
    u_ref[...] = seg(o, n_u)
    o += n_u
    q_ref[...] = (_rope(seg(o, n_q), cos, sa, sb) * Q_SCALE).astype(BF16)
    o += n_q
    k = _rope(seg(o, n_k), cos, sa, sb)
    k_ref[...] = k
    kb_ref[...] = k.astype(BF16)
    o += n_k
    v = seg(o, n_v)
    v_ref[...] = v
    vb_ref[...] = v.astype(BF16)
    if emit_vt:
        for p in range(n_v // LANES):
            vt_ref[0, p, 0:LANES, :] = v[:, p * LANES:(p + 1) * LANES].T.astype(BF16)
            vt_ref[0, p, LANES:, :] = jnp.ones((VT_ROWS - LANES, v.shape[0]), BF16)
    o += n_v
    qi_ref[...] = _rope(seg(o, n_qi), cos, sa, sb).astype(BF16)
    o += n_qi
    tail = seg(o, LANES)
    ki = _rope(tail, cos, sa, sb)[:, :IDX_DIM]
    ki_ref[...] = ki
    kib_ref[...] = ki.astype(BF16)
    wi_ref[...] = pltpu.roll(tail, LANES - IDX_DIM, 1) * (IDX_HEADS ** -0.5 * IDX_DIM ** -0.5)


def _proj(x, g, w, tabs, *, tm, tab_blocks, sizes, emit_vt):
    n, d = x.shape
    n_u, n_q, n_k, n_v, n_qi = sizes
    row = lambda i: (i, 0)
    tab = lambda i: (i % tab_blocks, 0)
    outs = [(n_u, F32), (n_q, BF16), (n_k, F32), (n_v, F32), (n_k, BF16), (n_v, BF16), (n_qi, BF16),
            (IDX_DIM, F32), (IDX_DIM, BF16), (LANES, F32)]
    out_shape = [jax.ShapeDtypeStruct((n, c), t) for c, t in outs]
    out_specs = [pl.BlockSpec((tm, c), row) for c, _ in outs]
    if emit_vt:
        out_shape.append(jax.ShapeDtypeStruct((n // tm, n_v // LANES, VT_ROWS, tm), BF16))
        out_specs.append(pl.BlockSpec((1, n_v // LANES, VT_ROWS, tm), lambda i: (i, 0, 0, 0)))
    return pl.pallas_call(
        functools.partial(_proj_kernel, sizes=sizes, emit_vt=emit_vt),
        grid=(n // tm,),
        in_specs=[pl.BlockSpec((tm, d), row), pl.BlockSpec((1, d), lambda i: (0, 0)),
                  pl.BlockSpec(w.shape, lambda i: (0, 0)),
                  pl.BlockSpec((tm, LANES), tab), pl.BlockSpec((tm, LANES), tab), pl.BlockSpec((tm, LANES), tab)],
        out_specs=out_specs, out_shape=out_shape,
        compiler_params=pltpu.CompilerParams(dimension_semantics=("arbitrary",), vmem_limit_bytes=VMEM_LIMIT),
        name="proj",
    )(x, g, w, *tabs)


def _rope_tables(pos):
    half = ROT_DIM // 2
    inv = jnp.power(jnp.float32(ROPE_THETA), -jnp.arange(half, dtype=F32) * 2.0 / ROT_DIM)
    ang = pos.astype(F32)[:, None] * inv[None, :]
    cos, sin = jnp.cos(ang), jnp.sin(ang)
    t = pos.shape[0]
    one = jnp.ones((t, HEAD_DIM - ROT_DIM), F32)
    zero = jnp.zeros((t, HEAD_DIM - ROT_DIM), F32)
    zh = jnp.zeros((t, half), F32)
    reps = LANES // HEAD_DIM
    c = jnp.tile(jnp.concatenate([cos, cos, one], axis=1), (1, reps))
    sa = jnp.tile(jnp.concatenate([-sin, zh, zero], axis=1), (1, reps))
    sb = jnp.tile(jnp.concatenate([zh, sin, zero], axis=1), (1, reps))
    return c, sa, sb


def _pool_kernel(u_ref, halo_ref, pw_ref, ps_ref, o_ref, *, tm, pos0, zero_first):
    i = pl.program_id(1)
    u = u_ref[0]
    halo = halo_ref[0]
    if zero_first:
        halo = jnp.where(i == 0, 0.0, halo)
    ext = jnp.concatenate([halo, u], axis=0)
    t = pos0 + i * tm + _iota((tm, LANES), 0)
    outs = []
    for g, win in enumerate(POOL_WINDOWS):
        s = ext[:, g * LANES:(g + 1) * LANES]
        sh = 1
        while sh < win:
            s = s + pltpu.roll(s, sh, 0)
            sh *= 2
        cnt = jnp.minimum(win, t + 1).astype(F32)
        d = s[POOL_HALO:] / cnt - u[:, g * LANES:(g + 1) * LANES]
        y = jnp.dot(d.astype(BF16), pw_ref[g], preferred_element_type=F32)
        outs.append(y * ps_ref[:, g * LANES:(g + 1) * LANES])
    o_ref[0] = jnp.concatenate(outs, axis=1).astype(BF16)


def _pool(u3, halo3, pw, ps, *, tm, pos0, zero_first):
    b, t, width = u3.shape
    assert width == POOL_GROUPS * LANES and tm % SUBLANES == 0
    if zero_first:
        hb = tm // POOL_HALO
        halo_map = lambda bi, i: (bi, jnp.maximum(i * hb - 1, 0), 0)
    else:
        assert t == tm
        halo_map = lambda bi, i: (bi, 0, 0)
    return pl.pallas_call(
        functools.partial(_pool_kernel, tm=tm, pos0=pos0, zero_first=zero_first),
        grid=(b, t // tm),
        in_specs=[pl.BlockSpec((1, tm, width), lambda bi, i: (bi, i, 0)),
                  pl.BlockSpec((1, POOL_HALO, width), halo_map),
                  pl.BlockSpec(pw.shape, lambda bi, i: (0, 0, 0)),
                  pl.BlockSpec((1, width), lambda bi, i: (0, 0))],
        out_specs=pl.BlockSpec((1, tm, width), lambda bi, i: (bi, i, 0)),
        out_shape=jax.ShapeDtypeStruct((b, t, width), BF16),
        compiler_params=pltpu.CompilerParams(dimension_semantics=("arbitrary", "arbitrary")),
        name="pool",
    )(u3, halo3, pw, ps)


def _mlp_kernel(x_ref, ap_ref, aa_ref, wo_ref, g_ref, w1_ref, w2_ref, gf_ref, *out_refs, ff_chunk, final):
    n_p = ap_ref.shape[1]
    x1 = (x_ref[...]
          + jnp.dot(ap_ref[...], wo_ref[:n_p, :], preferred_element_type=F32)
          + jnp.dot(aa_ref[...], wo_ref[n_p:, :], preferred_element_type=F32))
    h = _rms(x1, g_ref[...]).astype(BF16)
    ff = None
    for c in range(w1_ref.shape[1] // ff_chunk):
        a = jnp.dot(h, w1_ref[:, c * ff_chunk:(c + 1) * ff_chunk], preferred_element_type=F32)
        a = jnp.square(jnp.maximum(a, 0.0)).astype(BF16)
        part = jnp.dot(a, w2_ref[c * ff_chunk:(c + 1) * ff_chunk, :], preferred_element_type=F32)
        ff = part if ff is None else ff + part
    x2 = ff + x1
    out_refs[0][...] = x2
    if final:
        out_refs[1][...] = _rms(x2, gf_ref[...])


def _mlp(x, ap, aa, wo, g, w1, w2, gf, *, tm, final):
    n, d = x.shape
    row = lambda i: (i, 0)
    full = lambda i: (0, 0)
    n_out = 2 if final else 1
    return pl.pallas_call(
        functools.partial(_mlp_kernel, ff_chunk=1024, final=final),
        grid=(n // tm,),
        in_specs=[pl.BlockSpec((tm, d), row), pl.BlockSpec((tm, ap.shape[1]), row), pl.BlockSpec((tm, aa.shape[1]), row),
                  pl.BlockSpec(wo.shape, full), pl.BlockSpec((1, d), full),
                  pl.BlockSpec(w1.shape, full), pl.BlockSpec(w2.shape, full), pl.BlockSpec((1, d), full)],
        out_specs=[pl.BlockSpec((tm, d), row)] * n_out,
        out_shape=[jax.ShapeDtypeStruct((n, d), F32)] * n_out,
        compiler_params=pltpu.CompilerParams(dimension_semantics=("arbitrary",), vmem_limit_bytes=VMEM_LIMIT),
        name="mlp",
    )(x, ap, aa, wo, g, w1, w2, gf)


def _tree(vals, op):
    vals = list(vals)
    while len(vals) > 1:
        vals = [op(vals[i], vals[i + 1]) if i + 1 < len(vals) else vals[i] for i in range(0, len(vals), 2)]
    return vals[0]


def _attn_prompt_kernel(q_ref, qi_ref, wi_ref, ki_ref, k_ref, vt_ref, a_ref,
                        keys_ref, acc_ref, wp_ref, bias_ref, s_ref, p_ref, *, kc, topk, nbits):
    j = pl.program_id(1)
    n_heads = q_ref.shape[1] // HEAD_DIM
    n_pairs = n_heads // 2
    n_sub = kc // SUB_KEYS
    nch = (j * Q_BLOCK) // kc + 1
    qpos = j * Q_BLOCK + _iota((1, LANES), 1)

    qi_t = [qi_ref[:, c * LANES:(c + 1) * LANES].astype(F32).T for c in range(IDX_HEADS * IDX_DIM // LANES)]
    per = LANES // IDX_DIM
    qi_w = jnp.concatenate(
        [qi_t[h // per][(h % per) * IDX_DIM:(h % per + 1) * IDX_DIM, :] for h in range(IDX_HEADS)],
        axis=1).astype(BF16)
    w_t = wi_ref[...].T
    zeros = jnp.zeros((HEAD_DIM, LANES), F32)
    for p in range(n_pairs):
        q_t = q_ref[:, p * LANES:(p + 1) * LANES].astype(F32).T
        top = jnp.concatenate([q_t[:HEAD_DIM], zeros], axis=1)
        bot = jnp.concatenate([zeros, q_t[HEAD_DIM:]], axis=1)
        wp_ref[p] = jnp.concatenate([top, bot], axis=0).astype(BF16)

    def score_chunk(c, carry):
        for sb in range(n_sub):
            r0 = pl.multiple_of(c * kc + sb * SUB_KEYS, SUB_KEYS)
            d = jnp.dot(ki_ref[pl.ds(r0, SUB_KEYS), :], qi_w, preferred_element_type=F32)
            sc = w_t[0:1, :] * jnp.maximum(d[:, :LANES], 0.0)
            for h in range(1, IDX_HEADS):
                sc = sc + w_t[h:h + 1, :] * jnp.maximum(d[:, h * LANES:(h + 1) * LANES], 0.0)
            kpos = r0 + _iota((SUB_KEYS, LANES), 0)
            keys_ref[pl.ds(r0, SUB_KEYS), :] = jnp.where(kpos <= qpos, _score_keys(sc), INT_MIN)
        return carry

    lax.fori_loop(0, nch, score_chunk, 0)

    n_acc = 8
    tiles = kc // SUBLANES

    def count(pred):
        def body(c, accs):
            r0 = pl.multiple_of(c * kc, kc)
            new = []
            for a in range(n_acc):
                hits = []
                for i in range(a * tiles // n_acc, (a + 1) * tiles // n_acc):
                    blk = keys_ref[pl.ds(r0 + i * SUBLANES, SUBLANES), :]
                    hits.append(jnp.where(pred(blk, r0 + i * SUBLANES), 1.0, 0.0))
                new.append(accs[a] + _tree(hits, jnp.add))
            return tuple(new)
        accs = lax.fori_loop(0, nch, body, tuple(jnp.zeros((SUBLANES, LANES), F32) for _ in range(n_acc)))
        return jnp.sum(_tree(accs, jnp.add), axis=0, keepdims=True)

    def count_ge(cand):
        return count(lambda blk, r0: blk >= cand)

    kf = float(topk)
    few = qpos < topk

    def pending(cnt):
        return jnp.max(jnp.where(jnp.logical_or(cnt == kf, few), 0.0, 1.0))

    def search_cond(st):
        i, _, _, pend = st
        return jnp.logical_and(i < 32, pend > 0.0)

    def bit_step(i, tc):
        t, cnt = tc
        cand = t ^ lax.shift_left(jnp.int32(1), 31 - i)
        c = count_ge(cand)
        ok = c >= kf
        return jnp.where(ok, cand, t), jnp.where(ok, c, cnt)

    def search_step(st):
        i, t, cnt, _ = st
        t, cnt = lax.fori_loop(i, i + SEARCH_GROUP, bit_step, (t, cnt))
        return i + SEARCH_GROUP, t, cnt, pending(cnt)

    cnt0 = jnp.full((1, LANES), 1.0, F32) * (nch * kc).astype(F32)
    _, t, cnt, _ = lax.while_loop(search_cond, search_step,
                                  (jnp.int32(0), jnp.full((1, LANES), INT_MIN, I32), cnt0, pending(cnt0)))

    has_tie = jnp.logical_and(cnt > kf, jnp.logical_not(few))

    @pl.when(jnp.max(jnp.where(has_tie, 1.0, 0.0)) > 0.0)
    def _():
        need = kf - count_ge(t + 1)

        def pos_step(i, x):
            cand = x | lax.shift_left(jnp.int32(1), nbits - 1 - i)
            g = count(lambda blk, r0: jnp.logical_and(blk == t, r0 + _iota((SUBLANES, LANES), 0) < cand))
            return jnp.where(g < need, cand, x)
        x = lax.fori_loop(0, nbits, pos_step, jnp.zeros((1, LANES), I32))

        def drop(c, carry):
            r0 = pl.multiple_of(c * kc, kc)
            blk = keys_ref[pl.ds(r0, kc), :]
            lose = jnp.logical_and(jnp.logical_and(has_tie, blk == t), r0 + _iota((kc, LANES), 0) > x)
            keys_ref[pl.ds(r0, kc), :] = jnp.where(lose, INT_MIN, blk)
            return carry
        lax.fori_loop(0, nch, drop, 0)

    t_sel = jnp.maximum(t, INT_MIN + 1)

    acc_ref[...] = jnp.zeros_like(acc_ref)

    def attend_chunk(c, carry):
        ms, ls = carry
        r0 = pl.multiple_of(c * kc, kc)
        for sb in range(n_sub):
            rows = slice(sb * SUB_KEYS, (sb + 1) * SUB_KEYS)
            blk = keys_ref[pl.ds(r0 + sb * SUB_KEYS, SUB_KEYS), :]
            bias_ref[rows, :] = jnp.where(blk >= t_sel, 0.0, NEG)
        ms_new, ls_new = [], []
        for p in range(n_pairs):
            s_ref[p] = jnp.dot(k_ref[pl.ds(r0, kc), p * LANES:(p + 1) * LANES], wp_ref[p],
                               preferred_element_type=F32)
        for p in range(n_pairs):
            alphas = []
            for hh in range(2):
                h = 2 * p + hh
                cols = slice(hh * LANES, (hh + 1) * LANES)
                tiles_max = _tree([s_ref[p, sb * SUB_KEYS:(sb + 1) * SUB_KEYS, cols]
                                   + bias_ref[sb * SUB_KEYS:(sb + 1) * SUB_KEYS, :] for sb in range(n_sub)],
                                  jnp.maximum)
                m_new = jnp.maximum(ms[h], jnp.max(tiles_max, axis=0, keepdims=True))
                alphas.append(jnp.exp2(ms[h] - m_new))
                ms_new.append(m_new)
                for sb in range(n_sub):
                    rows = slice(sb * SUB_KEYS, (sb + 1) * SUB_KEYS)
                    p_ref[p, rows, cols] = jnp.exp2(s_ref[p, rows, cols] + bias_ref[rows, :] - m_new).astype(BF16)
            o = jnp.dot(vt_ref[c, p], p_ref[p], preferred_element_type=F32)
            for hh in range(2):
                h = 2 * p + hh
                rows = slice(h * HEAD_DIM, (h + 1) * HEAD_DIM)
                cols = slice(hh * LANES, (hh + 1) * LANES)
                acc_ref[rows, :] = alphas[hh] * acc_ref[rows, :] + o[hh * HEAD_DIM:(hh + 1) * HEAD_DIM, cols]
                ls_new.append(alphas[hh] * ls[h] + o[LANES:LANES + 1, cols])
        return tuple(ms_new), tuple(ls_new)

    init = (tuple(jnp.full((1, LANES), NEG, F32) for _ in range(n_heads)),
            tuple(jnp.zeros((1, LANES), F32) for _ in range(n_heads)))
    _, ls = lax.fori_loop(0, nch, attend_chunk, init)

    for p in range(n_pairs):
        o_t = jnp.concatenate(
            [acc_ref[h * HEAD_DIM:(h + 1) * HEAD_DIM, :] / ls[h] for h in (2 * p, 2 * p + 1)], axis=0)
        a_ref[:, p * LANES:(p + 1) * LANES] = o_t.T.astype(BF16)


def _attn_prompt(q, qi, wi, kib, kb, vt, *, b, s, kc):
    n = q.shape[0]
    width = q.shape[1]
    assert n == b * s and s % kc == 0 and kc % Q_BLOCK == 0 and kc % SUB_KEYS == 0 and width % (2 * HEAD_DIM) == 0
    nq = s // Q_BLOCK
    nck = s // kc
    n_pairs = width // LANES
    topk = min(TOPK_MAX, s // 4)
    nbits = max(1, int(np.ceil(np.log2(s))))
    qrow = lambda bi, j: (bi * nq + j, 0)
    resident = pl.Buffered(1)
    return pl.pallas_call(
        functools.partial(_attn_prompt_kernel, kc=kc, topk=topk, nbits=nbits),
        grid=(b, nq),
        in_specs=[pl.BlockSpec((Q_BLOCK, width), qrow),
                  pl.BlockSpec((Q_BLOCK, qi.shape[1]), qrow),
                  pl.BlockSpec((Q_BLOCK, LANES), qrow),
                  pl.BlockSpec((s, IDX_DIM), lambda bi, j: (bi, 0), pipeline_mode=resident),
                  pl.BlockSpec((s, width), lambda bi, j: (bi, 0), pipeline_mode=resident),
                  pl.BlockSpec((nck, n_pairs, VT_ROWS, kc), lambda bi, j: (bi, 0, 0, 0), pipeline_mode=resident)],
        out_specs=pl.BlockSpec((Q_BLOCK, width), qrow),
        out_shape=jax.ShapeDtypeStruct((n, width), BF16),
        scratch_shapes=[pltpu.VMEM((s, LANES), I32),
                        pltpu.VMEM((width, LANES), F32),
                        pltpu.VMEM((n_pairs, LANES, 2 * LANES), BF16),
                        pltpu.VMEM((kc, LANES), F32),
                        pltpu.VMEM((n_pairs, kc, 2 * LANES), F32),
                        pltpu.VMEM((n_pairs, kc, 2 * LANES), BF16)],
        compiler_params=pltpu.CompilerParams(dimension_semantics=("arbitrary", "arbitrary"),
                                             vmem_limit_bytes=VMEM_LIMIT),
        name="attn_prompt",
    )(q, qi, wi, kib, kb, vt)


def _attn_sample_kernel(pt_ref, q_ref, qi_ref, wi_ref, kn_ref, vn_ref, kin_ref, ck_ref, cv_ref, ci_ref,
                        a_ref, kibuf, keys_ref, keysn_ref, kbuf, vbuf, acc_ref, x_ref, sem_i, sem_kv,
                        *, layer, n_pages, gp, topk, nbits, past_len):
    b = pl.program_id(0)
    t_new = q_ref.shape[1]
    width = q_ref.shape[2]
    n_heads = width // HEAD_DIM
    ng = n_pages // gp
    gl = gp * PAGE
    page_rows = PAGE * n_heads

    def heads_to_lanes(buf):
        return jnp.concatenate([buf[pl.ds(h, gl, stride=n_heads), :] for h in range(n_heads)], axis=1).astype(BF16)

    def ki_copy(p):
        return pltpu.make_async_copy(ci_ref.at[pt_ref[b, p], layer], kibuf.at[p], sem_i.at[0])

    def kv_copies(g, slot):
        cps = []
        for i in range(gp):
            page = pt_ref[b, g * gp + i]
            dst = pl.ds(i * page_rows, page_rows)
            cps.append(pltpu.make_async_copy(ck_ref.at[page, layer], kbuf.at[slot, dst], sem_kv.at[0, slot]))
            cps.append(pltpu.make_async_copy(cv_ref.at[page, layer], vbuf.at[slot, dst], sem_kv.at[1, slot]))
        return cps

    def start_ki(p, c):
        ki_copy(p).start()
        return c

    def wait_ki(p, c):
        ki_copy(p).wait()
        return c

    lax.fori_loop(0, n_pages, start_ki, 0)
    for cp in kv_copies(0, 0):
        cp.start()

    qi = qi_ref[0].astype(F32)
    qi_rows = jnp.concatenate([qi[:, h * IDX_DIM:(h + 1) * IDX_DIM] for h in range(IDX_HEADS)], axis=0).astype(BF16)
    wi = wi_ref[0]
    w_col = jnp.concatenate([wi[:, h:h + 1] for h in range(IDX_HEADS)], axis=0)
    nt = (((1,), (1,)), ((), ()))

    def idx_scores(kmat):
        d = lax.dot_general(qi_rows, kmat, nt, preferred_element_type=F32)
        sc = w_col[0:t_new] * jnp.maximum(d[0:t_new], 0.0)
        for h in range(1, IDX_HEADS):
            sc = sc + w_col[h * t_new:(h + 1) * t_new] * jnp.maximum(d[h * t_new:(h + 1) * t_new], 0.0)
        return sc

    lax.fori_loop(0, n_pages, wait_ki, 0)

    def score_group(g, c):
        kmat = kibuf[pl.ds(g * gp, gp)].reshape(gl, IDX_DIM).astype(BF16)
        keys_ref[g] = _score_keys(idx_scores(kmat))
        return c

    lax.fori_loop(0, ng, score_group, 0)
    pad = jnp.zeros((LANES - t_new, IDX_DIM), BF16)
    sc_new = idx_scores(jnp.concatenate([kin_ref[0], pad], axis=0))
    new_ok = _iota((t_new, LANES), 1) <= _iota((t_new, LANES), 0)
    keysn_ref[...] = jnp.where(new_ok, _score_keys(sc_new), INT_MIN)

    def count(pred_past, pred_new):
        acc = jnp.zeros((t_new, gl), F32)
        for g in range(ng):
            acc = acc + jnp.where(pred_past(keys_ref[g], g), 1.0, 0.0)
        tot = jnp.sum(acc, axis=1, keepdims=True)
        return tot + jnp.sum(jnp.where(pred_new(keysn_ref[...]), 1.0, 0.0), axis=1, keepdims=True)

    def count_ge(cand):
        return count(lambda blk, g: blk >= cand, lambda blk: blk >= cand)

    kf = float(topk)
    t = jnp.where(count_ge(jnp.zeros((t_new, 1), I32)) >= kf, 0, INT_MIN).astype(I32)

    def bit_step(i, t):
        cand = t | lax.shift_left(jnp.int32(1), 30 - i)
        return jnp.where(count_ge(cand) >= kf, cand, t)

    t = lax.fori_loop(0, 31, bit_step, t)

    c_gt = count_ge(t + 1)
    c_ge = count_ge(t)
    need = kf - c_gt
    full_rows = t > INT_MIN
    has_tie = jnp.logical_and(c_ge - c_gt > need, full_rows)
    x_ref[...] = jnp.broadcast_to(jnp.where(full_rows, INT_MAX, -1).astype(I32), x_ref.shape)
    lane_g = _iota((t_new, gl), 1)
    lane_n = past_len + _iota((t_new, LANES), 1)

    @pl.when(jnp.max(jnp.where(has_tie, 1.0, 0.0)) > 0.0)
    def _():
        def pos_step(i, x):
            cand = x | lax.shift_left(jnp.int32(1), nbits - 1 - i)
            g = count(lambda blk, g: jnp.logical_and(blk == t, g * gl + lane_g < cand),
                      lambda blk: jnp.logical_and(blk == t, lane_n < cand))
            return jnp.where(g < need, cand, x)
        xs = lax.fori_loop(0, nbits, pos_step, jnp.zeros((t_new, 1), I32))
        x_ref[...] = jnp.broadcast_to(jnp.where(has_tie, xs, x_ref[:, 0:1]), x_ref.shape)

    x = x_ref[:, 0:1]

    rows = n_heads * t_new
    q = q_ref[0].astype(F32)
    own = _iota((rows, width), 0) // t_new == _iota((rows, width), 1) // HEAD_DIM
    qbd = jnp.where(own, jnp.concatenate([q] * n_heads, axis=0), 0.0).astype(BF16)

    def attend(kmat, vmat, blk, kpos, carry):
        m, l = carry
        sel = jnp.logical_or(blk > t, jnp.logical_and(blk == t, kpos <= x))
        bias = jnp.where(sel, 0.0, NEG)
        s = lax.dot_general(qbd, kmat, nt, preferred_element_type=F32) + jnp.concatenate([bias] * n_heads, axis=0)
        m_new = jnp.maximum(m, jnp.max(s, axis=1, keepdims=True))
        alpha = jnp.exp2(m - m_new)
        pr = jnp.exp2(s - m_new)
        l_new = alpha * l + jnp.sum(pr, axis=1, keepdims=True)
        acc_ref[...] = alpha * acc_ref[...] + jnp.dot(pr.astype(BF16), vmat, preferred_element_type=F32)
        return m_new, l_new

    acc_ref[...] = jnp.zeros_like(acc_ref)

    def group_step(g, carry):
        slot = g % 2

        @pl.when(g + 1 < ng)
        def _():
            for cp in kv_copies(g + 1, 1 - slot):
                cp.start()

        for cp in kv_copies(g, slot):
            cp.wait()
        kmat, vmat = heads_to_lanes(kbuf.at[slot]), heads_to_lanes(vbuf.at[slot])
        return attend(kmat, vmat, keys_ref[g], g * gl + lane_g, carry)

    carry = (jnp.full((rows, 1), NEG, F32), jnp.zeros((rows, 1), F32))
    carry = lax.fori_loop(0, ng, group_step, carry)
    padw = jnp.zeros((LANES - t_new, width), BF16)
    _, l = attend(jnp.concatenate([kn_ref[0], padw], axis=0), jnp.concatenate([vn_ref[0], padw], axis=0),
                  keysn_ref[...], lane_n, carry)

    o = jnp.where(own, acc_ref[...] / l, 0.0)
    out = o[0:t_new]
    for h in range(1, n_heads):
        out = out + o[h * t_new:(h + 1) * t_new]
    a_ref[0] = out.astype(BF16)


def _attn_sample(page_table, q, qi, wi, kn, vn, kin, ck, cv, ci, *, layer, gp):
    nb, t_new, width = q.shape
    n_pages = page_table.shape[1]
    page_rows = PAGE * (width // HEAD_DIM)
    assert t_new == SUBLANES and n_pages % gp == 0 and ck.shape[2:] == (page_rows, HEAD_DIM)
    past_len = n_pages * PAGE
    topk = min(TOPK_MAX, (past_len + t_new) // 4)
    nbits = max(1, int(np.ceil(np.log2(past_len + LANES))))
    ng = n_pages // gp
    gl = gp * PAGE
    rows = (width // HEAD_DIM) * t_new
    per_b = lambda c: pl.BlockSpec((1, t_new, c), lambda bi, pt: (bi, 0, 0))
    any_spec = pl.BlockSpec(memory_space=pl.ANY)
    grid_spec = pltpu.PrefetchScalarGridSpec(
        num_scalar_prefetch=1,
        grid=(nb,),
        in_specs=[per_b(width), per_b(qi.shape[2]), per_b(LANES), per_b(width), per_b(width), per_b(IDX_DIM),
                  any_spec, any_spec, any_spec],
        out_specs=per_b(width),
        scratch_shapes=[pltpu.VMEM((n_pages, PAGE, IDX_DIM), F32),
                        pltpu.VMEM((ng, t_new, gl), I32),
                        pltpu.VMEM((t_new, LANES), I32),
                        pltpu.VMEM((2, gp * page_rows, HEAD_DIM), F32),
                        pltpu.VMEM((2, gp * page_rows, HEAD_DIM), F32),
                        pltpu.VMEM((rows, width), F32),
                        pltpu.VMEM((t_new, LANES), I32),
                        pltpu.SemaphoreType.DMA((1,)),
                        pltpu.SemaphoreType.DMA((2, 2))])
    return pl.pallas_call(
        functools.partial(_attn_sample_kernel, layer=layer, n_pages=n_pages, gp=gp, topk=topk, nbits=nbits,
                          past_len=past_len),
        grid_spec=grid_spec,
        out_shape=jax.ShapeDtypeStruct((nb, t_new, width), BF16),
        compiler_params=pltpu.CompilerParams(dimension_semantics=("arbitrary",), vmem_limit_bytes=VMEM_LIMIT),
        name="attn_sample",
    )(page_table, q, qi, wi, kn, vn, kin, ck, cv, ci)


def kernel(x_prompt, x_sample, cache_k, cache_v, cache_idx_k, state_pool, page_table, w_in, w_out, pool_w,
           pool_scale, norm_mix, norm_ffn, w_ff1, w_ff2, norm_final):
    b, s, d = x_prompt.shape
    nb, t_new, _ = x_sample.shape
    depth = w_in.shape[0]
    n_pool = pool_scale.shape[1]
    n_kv_heads, head_dim = cache_k.shape[3], cache_k.shape[4]
    n_k = n_kv_heads * head_dim
    n_q = w_out.shape[1] - n_pool
    sizes = (n_pool, n_q, n_k, n_k, IDX_HEADS * IDX_DIM)
    d_in = w_in.shape[2]
    assert head_dim == HEAD_DIM and d_in == sum(sizes) + IDX_DIM + IDX_HEADS
    past_len = page_table.shape[1] * PAGE
    kc = min(512, s)

    d_pad = sum(sizes) + LANES
    w_in_b = jnp.pad(w_in, ((0, 0), (0, 0), (0, d_pad - d_in))).astype(BF16)
    w_out_b, w1_b, w2_b, pw_b = (a.astype(BF16) for a in (w_out, w_ff1, w_ff2, pool_w))
    tabs_p = _rope_tables(jnp.arange(s))
    tabs_s = tuple(jnp.tile(a, (nb, 1)) for a in _rope_tables(past_len + jnp.arange(t_new)))
    ck = cache_k.reshape(cache_k.shape[0], depth, PAGE * n_kv_heads, head_dim)
    cv = cache_v.reshape(cache_v.shape[0], depth, PAGE * n_kv_heads, head_dim)
    state_halo = jnp.pad(state_pool, ((0, 0), (0, 0), (POOL_HALO - state_pool.shape[2], 0), (0, 0)))
    gf = norm_final.reshape(1, d)

    xp = x_prompt.reshape(b * s, d)
    xs = x_sample.reshape(nb * t_new, d)
    kp_l, vp_l, ip_l, pp_l, ks_l, vs_l, is_l, ps_l = ([] for _ in range(8))
    yp = ys = None
    for l in range(depth):
        final = l == depth - 1
        g_mix, g_ffn = norm_mix[l].reshape(1, d), norm_ffn[l].reshape(1, d)
        scale = pool_scale[l].reshape(1, n_pool)

        u, q, k, v, kb, _, qi, ki, kib, wi, vt = _proj(xp, g_mix, w_in_b[l], tabs_p, tm=kc, tab_blocks=s // kc,
                                                        sizes=sizes, emit_vt=True)
        u3 = u.reshape(b, s, n_pool)
        a_pool = _pool(u3, u3, pw_b[l], scale, tm=kc, pos0=0, zero_first=True).reshape(b * s, n_pool)
        a_attn = _attn_prompt(q, qi, wi, kib, kb, vt, b=b, s=s, kc=kc)
        res = _mlp(xp, a_pool, a_attn, w_out_b[l], g_ffn, w1_b[l], w2_b[l], gf, tm=kc, final=final)
        xp = res[0]
        if final:
            yp = res[1]
        kp_l.append(k.reshape(b, s, n_kv_heads, head_dim))
        vp_l.append(v.reshape(b, s, n_kv_heads, head_dim))
        ip_l.append(ki.reshape(b, s, IDX_DIM))
        pp_l.append(u3[:, s - state_pool.shape[2]:])

        n_s = nb * t_new
        u, q, k, v, kb, vb, qi, ki, kib, wi = _proj(xs, g_mix, w_in_b[l], tabs_s, tm=n_s, tab_blocks=1,
                                                     sizes=sizes, emit_vt=False)
        u3 = u.reshape(nb, t_new, n_pool)
        a_pool = _pool(u3, state_halo[l], pw_b[l], scale, tm=t_new, pos0=past_len, zero_first=False)
        r3 = lambda a: a.reshape(nb, t_new, a.shape[1])
        a_attn = _attn_sample(page_table, r3(q), r3(qi), r3(wi), r3(kb), r3(vb), r3(kib), ck, cv, cache_idx_k,
                              layer=l, gp=min(8, page_table.shape[1]))
        res = _mlp(xs, a_pool.reshape(n_s, n_pool), a_attn.reshape(n_s, n_q), w_out_b[l], g_ffn, w1_b[l], w2_b[l],
                   gf, tm=n_s, final=final)
        xs = res[0]
        if final:
            ys = res[1]
        ks_l.append(k.reshape(nb, t_new, n_kv_heads, head_dim))
        vs_l.append(v.reshape(nb, t_new, n_kv_heads, head_dim))
        is_l.append(ki.reshape(nb, t_new, IDX_DIM))
        ps_l.append(jnp.concatenate([state_pool[l], u3], axis=1)[:, t_new:])

    return (yp.reshape(b, s, d), ys.reshape(nb, t_new, d),
            jnp.stack(kp_l, axis=1), jnp.stack(vp_l, axis=1), jnp.stack(ip_l, axis=1), jnp.stack(pp_l, axis=0),
            jnp.stack(ks_l, axis=1), jnp.stack(vs_l, axis=1), jnp.stack(is_l, axis=1), jnp.stack(ps_l, axis=0))
```

```python
import functools

import jax
import jax.numpy as jnp
import numpy as np
from jax import lax
from jax.experimental import pallas as pl
from jax.experimental.pallas import tpu as pltpu

F32 = jnp.float32
BF16 = jnp.bfloat16
I32 = jnp.int32

POOL_WINDOWS = (2, 4, 8, 16)
POOL_GROUPS = len(POOL_WINDOWS)
POOL_HALO = 16
HEAD_DIM = 64
ROT_DIM = 16
IDX_HEADS = 4
IDX_DIM = 64
TOPK_MAX = 256
ROPE_THETA = 500000.0
EPS = 1e-6
PAGE = 128
Q_BLOCK = 128
SUB_KEYS = 128
SEARCH_GROUP = 4

LANES = 128
SUBLANES = 8
VMEM_LIMIT = 56 * 1024 * 1024

BF16_ROWS = 16
VT_ROWS = LANES + BF16_ROWS
Q_SCALE = HEAD_DIM ** -0.5 * float(np.log2(np.e))

INT_MIN = -2147483648
INT_MAX = 2147483647
NEG = -1e30


def _iota(shape, dim):
    return lax.broadcasted_iota(I32, shape, dim)


def _rms(x, g):
    r = lax.rsqrt(jnp.mean(x * x, axis=-1, keepdims=True) + EPS)
    return (x * r) * g


def _score_keys(sc, pos, span):
    b = lax.bitcast_convert_type(sc, I32)
    m = b >> 31
    key = (b ^ (m & INT_MAX)) + (m & (1 - span))
    return jnp.where(sc == 0.0, -pos, key)


def _rope(seg, cos, sa, sb):
    outs = []
    for c in range(seg.shape[1] // LANES):
        s = seg[:, c * LANES:(c + 1) * LANES]
        outs.append(s * cos + pltpu.roll(s, LANES - ROT_DIM // 2, 1) * sa + pltpu.roll(s, ROT_DIM // 2, 1) * sb)
    return outs[0] if len(outs) == 1 else jnp.concatenate(outs, axis=1)


def _proj_kernel(x_ref, g_ref, w_ref, cos_ref, sa_ref, sb_ref, *out_refs, sizes, emit_vt):
    if emit_vt:
        u_ref, q_ref, k_ref, v_ref, kb_ref, vb_ref, qi_ref, ki_ref, kib_ref, wi_ref, vt_ref = out_refs
    else:
        u_ref, q_ref, k_ref, v_ref, kb_ref, vb_ref, qi_ref, ki_ref, kib_ref, wi_ref = out_refs
    n_u, n_q, n_k, n_v, n_qi = sizes
    h = _rms(x_ref[...], g_ref[...]).astype(BF16)
    cos, sa, sb = cos_ref[...], sa_ref[...], sb_ref[...]

    def seg(a, n):
        return jnp.dot(h, w_ref[:, a:a + n], preferred_element_type=F32)

    o = 0
    u_ref[...] = seg(o, n_u)
    o += n_u
    q_ref[...] = (_rope(seg(o, n_q), cos, sa, sb) * Q_SCALE).astype(BF16)
    o += n_q
    k = _rope(seg(o, n_k), cos, sa, sb)
    k_ref[...] = k
    kb_ref[...] = k.astype(BF16)
    o += n_k
    v = seg(o, n_v)
    v_ref[...] = v
    vb_ref[...] = v.astype(BF16)
    if emit_vt:
        for p in range(n_v // LANES):
            vt_ref[0, p, 0:LANES, :] = v[:, p * LANES:(p + 1) * LANES].T.astype(BF16)
            vt_ref[0, p, LANES:, :] = jnp.ones((VT_ROWS - LANES, v.shape[0]), BF16)
    o += n_v
    qi_ref[...] = _rope(seg(o, n_qi), cos, sa, sb).astype(BF16)
    o += n_qi
    tail = seg(o, LANES)
    ki = _rope(tail, cos, sa, sb)[:, :IDX_DIM]
    ki_ref[...] = ki
    kib_ref[...] = ki.astype(BF16)
    wi_ref[...] = pltpu.roll(tail, LANES - IDX_DIM, 1) * (IDX_HEADS ** -0.5 * IDX_DIM ** -0.5)


def _proj(x, g, w, tabs, *, tm, tab_blocks, sizes, emit_vt):
    n, d = x.shape
    n_u, n_q, n_k, n_v, n_qi = sizes
    row = lambda i: (i, 0)
    tab = lambda i: (i % tab_blocks, 0)
    outs = [(n_u, F32), (n_q, BF16), (n_k, F32), (n_v, F32), (n_k, BF16), (n_v, BF16), (n_qi, BF16),
            (IDX_DIM, F32), (IDX_DIM, BF16), (LANES, F32)]
    out_shape = [jax.ShapeDtypeStruct((n, c), t) for c, t in outs]
    out_specs = [pl.BlockSpec((tm, c), row) for c, _ in outs]
    if emit_vt:
        out_shape.append(jax.ShapeDtypeStruct((n // tm, n_v // LANES, VT_ROWS, tm), BF16))
        out_specs.append(pl.BlockSpec((1, n_v // LANES, VT_ROWS, tm), lambda i: (i, 0, 0, 0)))
    return pl.pallas_call(
        functools.partial(_proj_kernel, sizes=sizes, emit_vt=emit_vt),
        grid=(n // tm,),
        in_specs=[pl.BlockSpec((tm, d), row), pl.BlockSpec((1, d), lambda i: (0, 0)),
                  pl.BlockSpec(w.shape, lambda i: (0, 0)),
                  pl.BlockSpec((tm, LANES), tab), pl.BlockSpec((tm, LANES), tab), pl.BlockSpec((tm, LANES), tab)],
        out_specs=out_specs, out_shape=out_shape,
        compiler_params=pltpu.CompilerParams(dimension_semantics=("arbitrary",), vmem_limit_bytes=VMEM_LIMIT),
        name="proj",
    )(x, g, w, *tabs)


def _rope_tables(pos):
    half = ROT_DIM // 2
    inv = jnp.power(jnp.float32(ROPE_THETA), -jnp.arange(half, dtype=F32) * 2.0 / ROT_DIM)
    ang = pos.astype(F32)[:, None] * inv[None, :]
    cos, sin = jnp.cos(ang), jnp.sin(ang)
    t = pos.shape[0]
    one = jnp.ones((t, HEAD_DIM - ROT_DIM), F32)
    zero = jnp.zeros((t, HEAD_DIM - ROT_DIM), F32)
    zh = jnp.zeros((t, half), F32)
    reps = LANES // HEAD_DIM
    c = jnp.tile(jnp.concatenate([cos, cos, one], axis=1), (1, reps))
    sa = jnp.tile(jnp.concatenate([-sin, zh, zero], axis=1), (1, reps))
    sb = jnp.tile(jnp.concatenate([zh, sin, zero], axis=1), (1, reps))
    return c, sa, sb


def _pool_kernel(u_ref, halo_ref, pw_ref, ps_ref, o_ref, *, tm, pos0, zero_first):
    i = pl.program_id(1)
    u = u_ref[0]
    halo = halo_ref[0]
    if zero_first:
        halo = jnp.where(i == 0, 0.0, halo)
    ext = jnp.concatenate([halo, u], axis=0)
    t = pos0 + i * tm + _iota((tm, LANES), 0)
    outs = []
    for g, win in enumerate(POOL_WINDOWS):
        s = ext[:, g * LANES:(g + 1) * LANES]
        sh = 1
        while sh < win:
            s = s + pltpu.roll(s, sh, 0)
            sh *= 2
        cnt = jnp.minimum(win, t + 1).astype(F32)
        d = s[POOL_HALO:] / cnt - u[:, g * LANES:(g + 1) * LANES]
        y = jnp.dot(d.astype(BF16), pw_ref[g], preferred_element_type=F32)
        outs.append(y * ps_ref[:, g * LANES:(g + 1) * LANES])
    o_ref[0] = jnp.concatenate(outs, axis=1).astype(BF16)


def _pool(u3, halo3, pw, ps, *, tm, pos0, zero_first):
    b, t, width = u3.shape
    assert width == POOL_GROUPS * LANES and tm % SUBLANES == 0
    if zero_first:
        hb = tm // POOL_HALO
        halo_map = lambda bi, i: (bi, jnp.maximum(i * hb - 1, 0), 0)
    else:
        assert t == tm
        halo_map = lambda bi, i: (bi, 0, 0)
    return pl.pallas_call(
        functools.partial(_pool_kernel, tm=tm, pos0=pos0, zero_first=zero_first),
        grid=(b, t // tm),
        in_specs=[pl.BlockSpec((1, tm, width), lambda bi, i: (bi, i, 0)),
                  pl.BlockSpec((1, POOL_HALO, width), halo_map),
                  pl.BlockSpec(pw.shape, lambda bi, i: (0, 0, 0)),
                  pl.BlockSpec((1, width), lambda bi, i: (0, 0))],
        out_specs=pl.BlockSpec((1, tm, width), lambda bi, i: (bi, i, 0)),
        out_shape=jax.ShapeDtypeStruct((b, t, width), BF16),
        compiler_params=pltpu.CompilerParams(dimension_semantics=("arbitrary", "arbitrary")),
        name="pool",
    )(u3, halo3, pw, ps)


def _mlp_kernel(x_ref, ap_ref, aa_ref, wo_ref, g_ref, w1_ref, w2_ref, gf_ref, *out_refs, ff_chunk, final):
    n_p = ap_ref.shape[1]
    x1 = (x_ref[...]
          + jnp.dot(ap_ref[...], wo_ref[:n_p, :], preferred_element_type=F32)
          + jnp.dot(aa_ref[...], wo_ref[n_p:, :], preferred_element_type=F32))
    h = _rms(x1, g_ref[...]).astype(BF16)
    ff = None
    for c in range(w1_ref.shape[1] // ff_chunk):
        a = jnp.dot(h, w1_ref[:, c * ff_chunk:(c + 1) * ff_chunk], preferred_element_type=F32)
        a = jnp.square(jnp.maximum(a, 0.0)).astype(BF16)
        part = jnp.dot(a, w2_ref[c * ff_chunk:(c + 1) * ff_chunk, :], preferred_element_type=F32)
        ff = part if ff is None else ff + part
    x2 = ff + x1
    out_refs[0][...] = x2
    if final:
        out_refs[1][...] = _rms(x2, gf_ref[...])


def _mlp(x, ap, aa, wo, g, w1, w2, gf, *, tm, final):
    n, d = x.shape
    row = lambda i: (i, 0)
    full = lambda i: (0, 0)
    n_out = 2 if final else 1
    return pl.pallas_call(
        functools.partial(_mlp_kernel, ff_chunk=1024, final=final),
        grid=(n // tm,),
        in_specs=[pl.BlockSpec((tm, d), row), pl.BlockSpec((tm, ap.shape[1]), row), pl.BlockSpec((tm, aa.shape[1]), row),
                  pl.BlockSpec(wo.shape, full), pl.BlockSpec((1, d), full),
                  pl.BlockSpec(w1.shape, full), pl.BlockSpec(w2.shape, full), pl.BlockSpec((1, d), full)],
        out_specs=[pl.BlockSpec((tm, d), row)] * n_out,
        out_shape=[jax.ShapeDtypeStruct((n, d), F32)] * n_out,
        compiler_params=pltpu.CompilerParams(dimension_semantics=("arbitrary",), vmem_limit_bytes=VMEM_LIMIT),
        name="mlp",
    )(x, ap, aa, wo, g, w1, w2, gf)


def _tree(vals, op):
    vals = list(vals)
    while len(vals) > 1:
        vals = [op(vals[i], vals[i + 1]) if i + 1 < len(vals) else vals[i] for i in range(0, len(vals), 2)]
    return vals[0]


def _attn_prompt_kernel(q_ref, qi_ref, wi_ref, ki_ref, k_ref, vt_ref, a_ref,
                        keys_ref, acc_ref, wp_ref, bias_ref, s_ref, p_ref, *, kc, topk, nbits):
    j = pl.program_id(1)
    n_heads = q_ref.shape[1] // HEAD_DIM
    n_pairs = n_heads // 2
    n_sub = kc // SUB_KEYS
    nch = (j * Q_BLOCK) // kc + 1
    qpos = j * Q_BLOCK + _iota((1, LANES), 1)

    qi_t = [qi_ref[:, c * LANES:(c + 1) * LANES].astype(F32).T for c in range(IDX_HEADS * IDX_DIM // LANES)]
    per = LANES // IDX_DIM
    qi_w = jnp.concatenate(
        [qi_t[h // per][(h % per) * IDX_DIM:(h % per + 1) * IDX_DIM, :] for h in range(IDX_HEADS)],
        axis=1).astype(BF16)
    w_t = wi_ref[...].T
    zeros = jnp.zeros((HEAD_DIM, LANES), F32)
    for p in range(n_pairs):
        q_t = q_ref[:, p * LANES:(p + 1) * LANES].astype(F32).T
        top = jnp.concatenate([q_t[:HEAD_DIM], zeros], axis=1)
        bot = jnp.concatenate([zeros, q_t[HEAD_DIM:]], axis=1)
        wp_ref[p] = jnp.concatenate([top, bot], axis=0).astype(BF16)

    def score_chunk(c, carry):
        for sb in range(n_sub):
            r0 = pl.multiple_of(c * kc + sb * SUB_KEYS, SUB_KEYS)
            d = jnp.dot(ki_ref[pl.ds(r0, SUB_KEYS), :], qi_w, preferred_element_type=F32)
            sc = w_t[0:1, :] * jnp.maximum(d[:, :LANES], 0.0)
            for h in range(1, IDX_HEADS):
                sc = sc + w_t[h:h + 1, :] * jnp.maximum(d[:, h * LANES:(h + 1) * LANES], 0.0)
            kpos = r0 + _iota((SUB_KEYS, LANES), 0)
            keys_ref[pl.ds(r0, SUB_KEYS), :] = jnp.where(kpos <= qpos, _score_keys(sc, kpos, 1 << nbits), INT_MIN)
        return carry

    lax.fori_loop(0, nch, score_chunk, 0)

    n_acc = 8
    tiles = kc // SUBLANES

    def count(pred):
        def body(c, accs):
            r0 = pl.multiple_of(c * kc, kc)
            new = []
            for a in range(n_acc):
                hits = []
                for i in range(a * tiles // n_acc, (a + 1) * tiles // n_acc):
                    blk = keys_ref[pl.ds(r0 + i * SUBLANES, SUBLANES), :]
                    hits.append(jnp.where(pred(blk, r0 + i * SUBLANES), 1.0, 0.0))
                new.append(accs[a] + _tree(hits, jnp.add))
            return tuple(new)
        accs = lax.fori_loop(0, nch, body, tuple(jnp.zeros((SUBLANES, LANES), F32) for _ in range(n_acc)))
        return jnp.sum(_tree(accs, jnp.add), axis=0, keepdims=True)

    def count_ge(cand):
        return count(lambda blk, r0: blk >= cand)

    kf = float(topk)
    few = qpos < topk

    def pending(cnt):
        return jnp.max(jnp.where(jnp.logical_or(cnt == kf, few), 0.0, 1.0))

    def search_cond(st):
        i, _, _, pend = st
        return jnp.logical_and(i < 32, pend > 0.0)

    def bit_step(i, tc):
        t, cnt = tc
        cand = t ^ lax.shift_left(jnp.int32(1), 31 - i)
        c = count_ge(cand)
        ok = c >= kf
        return jnp.where(ok, cand, t), jnp.where(ok, c, cnt)

    def search_step(st):
        i, t, cnt, _ = st
        t, cnt = lax.fori_loop(i, i + SEARCH_GROUP, bit_step, (t, cnt))
        return i + SEARCH_GROUP, t, cnt, pending(cnt)

    cnt0 = jnp.full((1, LANES), 1.0, F32) * (nch * kc).astype(F32)
    _, t, cnt, _ = lax.while_loop(search_cond, search_step,
                                  (jnp.int32(0), jnp.full((1, LANES), INT_MIN, I32), cnt0, pending(cnt0)))

    has_tie = jnp.logical_and(cnt > kf, jnp.logical_not(few))

    @pl.when(jnp.max(jnp.where(has_tie, 1.0, 0.0)) > 0.0)
    def _():
        need = kf - count_ge(t + 1)

        def pos_step(i, x):
            cand = x | lax.shift_left(jnp.int32(1), nbits - 1 - i)
            g = count(lambda blk, r0: jnp.logical_and(blk == t, r0 + _iota((SUBLANES, LANES), 0) < cand))
            return jnp.where(g < need, cand, x)
        x = lax.fori_loop(0, nbits, pos_step, jnp.zeros((1, LANES), I32))

        def drop(c, carry):
            r0 = pl.multiple_of(c * kc, kc)
            blk = keys_ref[pl.ds(r0, kc), :]
            lose = jnp.logical_and(jnp.logical_and(has_tie, blk == t), r0 + _iota((kc, LANES), 0) > x)
            keys_ref[pl.ds(r0, kc), :] = jnp.where(lose, INT_MIN, blk)
            return carry
        lax.fori_loop(0, nch, drop, 0)

    t_sel = jnp.maximum(t, INT_MIN + 1)

    acc_ref[...] = jnp.zeros_like(acc_ref)

    def attend_chunk(c, carry):
        ms, ls = carry
        r0 = pl.multiple_of(c * kc, kc)
        for sb in range(n_sub):
            rows = slice(sb * SUB_KEYS, (sb + 1) * SUB_KEYS)
            blk = keys_ref[pl.ds(r0 + sb * SUB_KEYS, SUB_KEYS), :]
            bias_ref[rows, :] = jnp.where(blk >= t_sel, 0.0, NEG)
        ms_new, ls_new = [], []
        for p in range(n_pairs):
            s_ref[p] = jnp.dot(k_ref[pl.ds(r0, kc), p * LANES:(p + 1) * LANES], wp_ref[p],
                               preferred_element_type=F32)
        for p in range(n_pairs):
            alphas = []
            for hh in range(2):
                h = 2 * p + hh
                cols = slice(hh * LANES, (hh + 1) * LANES)
                tiles_max = _tree([s_ref[p, sb * SUB_KEYS:(sb + 1) * SUB_KEYS, cols]
                                   + bias_ref[sb * SUB_KEYS:(sb + 1) * SUB_KEYS, :] for sb in range(n_sub)],
                                  jnp.maximum)
                m_new = jnp.maximum(ms[h], jnp.max(tiles_max, axis=0, keepdims=True))
                alphas.append(jnp.exp2(ms[h] - m_new))
                ms_new.append(m_new)
                for sb in range(n_sub):
                    rows = slice(sb * SUB_KEYS, (sb + 1) * SUB_KEYS)
                    p_ref[p, rows, cols] = jnp.exp2(s_ref[p, rows, cols] + bias_ref[rows, :] - m_new).astype(BF16)
            o = jnp.dot(vt_ref[c, p], p_ref[p], preferred_element_type=F32)
            for hh in range(2):
                h = 2 * p + hh
                rows = slice(h * HEAD_DIM, (h + 1) * HEAD_DIM)
                cols = slice(hh * LANES, (hh + 1) * LANES)
                acc_ref[rows, :] = alphas[hh] * acc_ref[rows, :] + o[hh * HEAD_DIM:(hh + 1) * HEAD_DIM, cols]
                ls_new.append(alphas[hh] * ls[h] + o[LANES:LANES + 1, cols])
        return tuple(ms_new), tuple(ls_new)

    init = (tuple(jnp.full((1, LANES), NEG, F32) for _ in range(n_heads)),
            tuple(jnp.zeros((1, LANES), F32) for _ in range(n_heads)))
    _, ls = lax.fori_loop(0, nch, attend_chunk, init)

    for p in range(n_pairs):
        o_t = jnp.concatenate(
            [acc_ref[h * HEAD_DIM:(h + 1) * HEAD_DIM, :] / ls[h] for h in (2 * p, 2 * p + 1)], axis=0)
        a_ref[:, p * LANES:(p + 1) * LANES] = o_t.T.astype(BF16)


def _attn_prompt(q, qi, wi, kib, kb, vt, *, b, s, kc):
    n = q.shape[0]
    width = q.shape[1]
    assert n == b * s and s % kc == 0 and kc % Q_BLOCK == 0 and kc % SUB_KEYS == 0 and width % (2 * HEAD_DIM) == 0
    nq = s // Q_BLOCK
    nck = s // kc
    n_pairs = width // LANES
    topk = min(TOPK_MAX, s // 4)
    nbits = max(1, int(np.ceil(np.log2(s))))
    qrow = lambda bi, j: (bi * nq + j, 0)
    resident = pl.Buffered(1)
    return pl.pallas_call(
        functools.partial(_attn_prompt_kernel, kc=kc, topk=topk, nbits=nbits),
        grid=(b, nq),
        in_specs=[pl.BlockSpec((Q_BLOCK, width), qrow),
                  pl.BlockSpec((Q_BLOCK, qi.shape[1]), qrow),
                  pl.BlockSpec((Q_BLOCK, LANES), qrow),
                  pl.BlockSpec((s, IDX_DIM), lambda bi, j: (bi, 0), pipeline_mode=resident),
                  pl.BlockSpec((s, width), lambda bi, j: (bi, 0), pipeline_mode=resident),
                  pl.BlockSpec((nck, n_pairs, VT_ROWS, kc), lambda bi, j: (bi, 0, 0, 0), pipeline_mode=resident)],
        out_specs=pl.BlockSpec((Q_BLOCK, width), qrow),
        out_shape=jax.ShapeDtypeStruct((n, width), BF16),
        scratch_shapes=[pltpu.VMEM((s, LANES), I32),
                        pltpu.VMEM((width, LANES), F32),
                        pltpu.VMEM((n_pairs, LANES, 2 * LANES), BF16),
                        pltpu.VMEM((kc, LANES), F32),
                        pltpu.VMEM((n_pairs, kc, 2 * LANES), F32),
                        pltpu.VMEM((n_pairs, kc, 2 * LANES), BF16)],
        compiler_params=pltpu.CompilerParams(dimension_semantics=("arbitrary", "arbitrary"),
                                             vmem_limit_bytes=VMEM_LIMIT),
        name="attn_prompt",
    )(q, qi, wi, kib, kb, vt)


def _attn_sample_kernel(pt_ref, q_ref, qi_ref, wi_ref, kn_ref, vn_ref, kin_ref, ck_ref, cv_ref, ci_ref,
                        a_ref, kibuf, keys_ref, keysn_ref, kbuf, vbuf, acc_ref, x_ref, sem_i, sem_kv,
                        *, layer, n_pages, gp, topk, nbits, past_len):
    b = pl.program_id(0)
    t_new = q_ref.shape[1]
    width = q_ref.shape[2]
    n_heads = width // HEAD_DIM
    ng = n_pages // gp
    gl = gp * PAGE

    def pages_to_lanes(buf, first):
        return jnp.concatenate([buf[first + i] for i in range(gp)], axis=1).astype(BF16)

    def transposed_new(rows_ref):
        x = rows_ref[0].astype(F32)
        c = x.shape[1]
        x = jnp.concatenate([x, jnp.zeros((LANES - t_new, c), F32)], axis=0)
        if c % LANES:
            x = jnp.concatenate([x, jnp.zeros((LANES, LANES - c % LANES), F32)], axis=1)
        return x.T[:c].astype(BF16)

    def ki_copy(p):
        return pltpu.make_async_copy(ci_ref.at[pt_ref[b, p], layer], kibuf.at[p], sem_i.at[0])

    def kv_copies(g, slot):
        cps = []
        for i in range(gp):
            page = pt_ref[b, g * gp + i]
            cps.append(pltpu.make_async_copy(ck_ref.at[page, layer], kbuf.at[slot, i], sem_kv.at[0, slot]))
            cps.append(pltpu.make_async_copy(cv_ref.at[page, layer], vbuf.at[slot, i], sem_kv.at[1, slot]))
        return cps

    def start_ki(p, c):
        ki_copy(p).start()
        return c

    def wait_ki(p, c):
        ki_copy(p).wait()
        return c

    lax.fori_loop(0, n_pages, start_ki, 0)
    for cp in kv_copies(0, 0):
        cp.start()

    qi = qi_ref[0].astype(F32)
    qi_rows = jnp.concatenate([qi[:, h * IDX_DIM:(h + 1) * IDX_DIM] for h in range(IDX_HEADS)], axis=0).astype(BF16)
    wi = wi_ref[0]
    w_col = jnp.concatenate([wi[:, h:h + 1] for h in range(IDX_HEADS)], axis=0)
    nt = (((1,), (1,)), ((), ()))
    lane_g = _iota((t_new, gl), 1)
    lane_n = past_len + _iota((t_new, LANES), 1)

    def idx_scores(kt):
        d = jnp.dot(qi_rows, kt, preferred_element_type=F32)
        sc = w_col[0:t_new] * jnp.maximum(d[0:t_new], 0.0)
        for h in range(1, IDX_HEADS):
            sc = sc + w_col[h * t_new:(h + 1) * t_new] * jnp.maximum(d[h * t_new:(h + 1) * t_new], 0.0)
        return sc

    lax.fori_loop(0, n_pages, wait_ki, 0)

    def score_group(g, c):
        keys_ref[g] = _score_keys(idx_scores(pages_to_lanes(kibuf, g * gp)), g * gl + lane_g, 1 << nbits)
        return c

    lax.fori_loop(0, ng, score_group, 0)
    sc_new = idx_scores(transposed_new(kin_ref))
    new_ok = _iota((t_new, LANES), 1) <= _iota((t_new, LANES), 0)
    keysn_ref[...] = jnp.where(new_ok, _score_keys(sc_new, lane_n, 1 << nbits), INT_MIN)

    def count(pred_past, pred_new):
        acc = jnp.zeros((t_new, gl), F32)
        for g in range(ng):
            acc = acc + jnp.where(pred_past(keys_ref[g], g), 1.0, 0.0)
        tot = jnp.sum(acc, axis=1, keepdims=True)
        return tot + jnp.sum(jnp.where(pred_new(keysn_ref[...]), 1.0, 0.0), axis=1, keepdims=True)

    def count_ge(cand):
        return count(lambda blk, g: blk >= cand, lambda blk: blk >= cand)

    kf = float(topk)
    t = jnp.where(count_ge(jnp.zeros((t_new, 1), I32)) >= kf, 0, INT_MIN).astype(I32)

    def bit_step(i, t):
        cand = t | lax.shift_left(jnp.int32(1), 30 - i)
        return jnp.where(count_ge(cand) >= kf, cand, t)

    t = lax.fori_loop(0, 31, bit_step, t)

    c_gt = count_ge(t + 1)
    c_ge = count_ge(t)
    need = kf - c_gt
    full_rows = t > INT_MIN
    has_tie = jnp.logical_and(c_ge - c_gt > need, full_rows)
    x_ref[...] = jnp.broadcast_to(jnp.where(full_rows, INT_MAX, -1).astype(I32), x_ref.shape)

    @pl.when(jnp.max(jnp.where(has_tie, 1.0, 0.0)) > 0.0)
    def _():
        def pos_step(i, x):
            cand = x | lax.shift_left(jnp.int32(1), nbits - 1 - i)
            g = count(lambda blk, g: jnp.logical_and(blk == t, g * gl + lane_g < cand),
                      lambda blk: jnp.logical_and(blk == t, lane_n < cand))
            return jnp.where(g < need, cand, x)
        xs = lax.fori_loop(0, nbits, pos_step, jnp.zeros((t_new, 1), I32))
        x_ref[...] = jnp.broadcast_to(jnp.where(has_tie, xs, x_ref[:, 0:1]), x_ref.shape)

    x = x_ref[:, 0:1]

    rows = n_heads * t_new
    q = q_ref[0].astype(F32)
    own = _iota((rows, width), 0) // t_new == _iota((rows, width), 1) // HEAD_DIM
    qbd = jnp.where(own, jnp.concatenate([q] * n_heads, axis=0), 0.0).astype(BF16)

    def attend(kt, vt, blk, kpos, carry):
        m, l = carry
        sel = jnp.logical_or(blk > t, jnp.logical_and(blk == t, kpos <= x))
        bias = jnp.where(sel, 0.0, NEG)
        s = jnp.dot(qbd, kt, preferred_element_type=F32) + jnp.concatenate([bias] * n_heads, axis=0)
        m_new = jnp.maximum(m, jnp.max(s, axis=1, keepdims=True))
        alpha = jnp.exp2(m - m_new)
        pr = jnp.exp2(s - m_new)
        l_new = alpha * l + jnp.sum(pr, axis=1, keepdims=True)
        pv = lax.dot_general(pr.astype(BF16), vt, nt, preferred_element_type=F32)
        acc_ref[...] = alpha * acc_ref[...] + pv
        return m_new, l_new

    acc_ref[...] = jnp.zeros_like(acc_ref)

    def group_step(g, carry):
        slot = g % 2

        @pl.when(g + 1 < ng)
        def _():
            for cp in kv_copies(g + 1, 1 - slot):
                cp.start()

        for cp in kv_copies(g, slot):
            cp.wait()
        kt, vt = pages_to_lanes(kbuf.at[slot], 0), pages_to_lanes(vbuf.at[slot], 0)
        return attend(kt, vt, keys_ref[g], g * gl + lane_g, carry)

    carry = (jnp.full((rows, 1), NEG, F32), jnp.zeros((rows, 1), F32))
    carry = lax.fori_loop(0, ng, group_step, carry)
    _, l = attend(transposed_new(kn_ref), transposed_new(vn_ref), keysn_ref[...], lane_n, carry)

    o = jnp.where(own, acc_ref[...] / l, 0.0)
    out = o[0:t_new]
    for h in range(1, n_heads):
        out = out + o[h * t_new:(h + 1) * t_new]
    a_ref[0] = out.astype(BF16)


def _attn_sample(page_table, q, qi, wi, kn, vn, kin, ck, cv, ci, *, layer, gp):
    nb, t_new, width = q.shape
    n_pages = page_table.shape[1]
    assert t_new == SUBLANES and n_pages % gp == 0
    assert ck.shape[2:] == (width, PAGE) and cv.shape == ck.shape and ci.shape[2:] == (IDX_DIM, PAGE)
    past_len = n_pages * PAGE
    topk = min(TOPK_MAX, (past_len + t_new) // 4)
    nbits = max(1, int(np.ceil(np.log2(past_len + LANES))))
    ng = n_pages // gp
    gl = gp * PAGE
    rows = (width // HEAD_DIM) * t_new
    per_b = lambda c: pl.BlockSpec((1, t_new, c), lambda bi, pt: (bi, 0, 0))
    any_spec = pl.BlockSpec(memory_space=pl.ANY)
    grid_spec = pltpu.PrefetchScalarGridSpec(
        num_scalar_prefetch=1,
        grid=(nb,),
        in_specs=[per_b(width), per_b(qi.shape[2]), per_b(LANES), per_b(width), per_b(width), per_b(IDX_DIM),
                  any_spec, any_spec, any_spec],
        out_specs=per_b(width),
        scratch_shapes=[pltpu.VMEM((n_pages, IDX_DIM, PAGE), F32),
                        pltpu.VMEM((ng, t_new, gl), I32),
                        pltpu.VMEM((t_new, LANES), I32),
                        pltpu.VMEM((2, gp, width, PAGE), F32),
                        pltpu.VMEM((2, gp, width, PAGE), F32),
                        pltpu.VMEM((rows, width), F32),
                        pltpu.VMEM((t_new, LANES), I32),
                        pltpu.SemaphoreType.DMA((1,)),
                        pltpu.SemaphoreType.DMA((2, 2))])
    return pl.pallas_call(
        functools.partial(_attn_sample_kernel, layer=layer, n_pages=n_pages, gp=gp, topk=topk, nbits=nbits,
                          past_len=past_len),
        grid_spec=grid_spec,
        out_shape=jax.ShapeDtypeStruct((nb, t_new, width), BF16),
        compiler_params=pltpu.CompilerParams(dimension_semantics=("arbitrary",), vmem_limit_bytes=VMEM_LIMIT),
        name="attn_sample",
    )(page_table, q, qi, wi, kn, vn, kin, ck, cv, ci)


def kernel(x_prompt, x_sample, cache_k, cache_v, cache_idx_k, state_pool, page_table, w_in, w_out, pool_w,
           pool_scale, norm_mix, norm_ffn, w_ff1, w_ff2, norm_final):
    b, s, d = x_prompt.shape
    nb, t_new, _ = x_sample.shape
    depth = w_in.shape[0]
    n_pool = pool_scale.shape[1]
    n_kv_heads, head_dim = cache_k.shape[3], cache_k.shape[4]
    n_k = n_kv_heads * head_dim
    n_q = w_out.shape[1] - n_pool
    sizes = (n_pool, n_q, n_k, n_k, IDX_HEADS * IDX_DIM)
    d_in = w_in.shape[2]
    assert head_dim == HEAD_DIM and d_in == sum(sizes) + IDX_DIM + IDX_HEADS
    past_len = page_table.shape[1] * PAGE
    kc = min(512, s)

    d_pad = sum(sizes) + LANES
    w_in_b = jnp.pad(w_in, ((0, 0), (0, 0), (0, d_pad - d_in))).astype(BF16)
    w_out_b, w1_b, w2_b, pw_b = (a.astype(BF16) for a in (w_out, w_ff1, w_ff2, pool_w))
    tabs_p = _rope_tables(jnp.arange(s))
    tabs_s = tuple(jnp.tile(a, (nb, 1)) for a in _rope_tables(past_len + jnp.arange(t_new)))
    ck = jnp.transpose(cache_k, (0, 1, 3, 4, 2)).reshape(cache_k.shape[0], depth, n_k, PAGE)
    cv = jnp.transpose(cache_v, (0, 1, 3, 4, 2)).reshape(cache_v.shape[0], depth, n_k, PAGE)
    ci = jnp.transpose(cache_idx_k, (0, 1, 3, 2))
    state_halo = jnp.pad(state_pool, ((0, 0), (0, 0), (POOL_HALO - state_pool.shape[2], 0), (0, 0)))
    gf = norm_final.reshape(1, d)

    xp = x_prompt.reshape(b * s, d)
    xs = x_sample.reshape(nb * t_new, d)
    kp_l, vp_l, ip_l, pp_l, ks_l, vs_l, is_l, ps_l = ([] for _ in range(8))
    yp = ys = None
    for l in range(depth):
        final = l == depth - 1
        g_mix, g_ffn = norm_mix[l].reshape(1, d), norm_ffn[l].reshape(1, d)
        scale = pool_scale[l].reshape(1, n_pool)

        u, q, k, v, kb, _, qi, ki, kib, wi, vt = _proj(xp, g_mix, w_in_b[l], tabs_p, tm=kc, tab_blocks=s // kc,
                                                        sizes=sizes, emit_vt=True)
        u3 = u.reshape(b, s, n_pool)
        a_pool = _pool(u3, u3, pw_b[l], scale, tm=kc, pos0=0, zero_first=True).reshape(b * s, n_pool)
        a_attn = _attn_prompt(q, qi, wi, kib, kb, vt, b=b, s=s, kc=kc)
        res = _mlp(xp, a_pool, a_attn, w_out_b[l], g_ffn, w1_b[l], w2_b[l], gf, tm=kc, final=final)
        xp = res[0]
        if final:
            yp = res[1]
        kp_l.append(k.reshape(b, s, n_kv_heads, head_dim))
        vp_l.append(v.reshape(b, s, n_kv_heads, head_dim))
        ip_l.append(ki.reshape(b, s, IDX_DIM))
        pp_l.append(u3[:, s - state_pool.shape[2]:])

        n_s = nb * t_new
        u, q, k, v, kb, vb, qi, ki, kib, wi = _proj(xs, g_mix, w_in_b[l], tabs_s, tm=n_s, tab_blocks=1,
                                                     sizes=sizes, emit_vt=False)
        u3 = u.reshape(nb, t_new, n_pool)
        a_pool = _pool(u3, state_halo[l], pw_b[l], scale, tm=t_new, pos0=past_len, zero_first=False)
        r3 = lambda a: a.reshape(nb, t_new, a.shape[1])
        a_attn = _attn_sample(page_table, r3(q), r3(qi), r3(wi), r3(kb), r3(vb), r3(kib), ck, cv, ci,
                              layer=l, gp=min(8, page_table.shape[1]))
        res = _mlp(xs, a_pool.reshape(n_s, n_pool), a_attn.reshape(n_s, n_q), w_out_b[l], g_ffn, w1_b[l], w2_b[l],
                   gf, tm=n_s, final=final)
        xs = res[0]
        if final:
            ys = res[1]
        ks_l.append(k.reshape(nb, t_new, n_kv_heads, head_dim))
        vs_l.append(v.reshape(nb, t_new, n_kv_heads, head_dim))
        is_l.append(ki.reshape(nb, t_new, IDX_DIM))
        ps_l.append(jnp.concatenate([state_pool[l], u3], axis=1)[:, t_new:])

    return (yp.reshape(b, s, d), ys.reshape(nb, t_new, d),
            jnp.stack(kp_l, axis=1), jnp.stack(vp_l, axis=1), jnp.stack(ip_l, axis=1), jnp.stack(pp_l, axis=0),
            jnp.stack(ks_l, axis=1), jnp.stack(vs_l, axis=1), jnp.stack(is_l, axis=1), jnp.stack(ps_l, axis=0))
```
